```python
import math, functools
import jax, jax.numpy as jnp
from jax import lax
import numpy as np

D_MODEL = 1024
BATCH = 2
SEQ = 8192
DEPTH = 1
DEC_BATCH = 128
DEC_SEQ = 4
PAST_LEN = 2048
PAGE_SIZE = 128

MIX_A = D_MODEL // 2
MIX_H = D_MODEL - MIX_A
A_HEAD_DIM = 64
A_HEADS = MIX_A // (2 * A_HEAD_DIM)
ROT_DIM = A_HEAD_DIM // 4
ROPE_THETA = 500000.0
H_EXPAND = 128
H_HEADS = MIX_H // H_EXPAND
D_FF = -(-8 * D_MODEL // (3 * 256)) * 256
N_IN = 3 * MIX_A + 4 * MIX_H
SPLITS = [MIX_A, 2 * MIX_A, 3 * MIX_A, 3 * MIX_A + MIX_H, 3 * MIX_A + 2 * MIX_H, 3 * MIX_A + 3 * MIX_H]
QBLK = 128
CHUNK = 64
EPS = 1e-6
SUBLN_EPS = 1e-5

kernel_name = "hymba_style_diffattn_hgrn2_step"


def rmsnorm(x, w, eps=EPS):
    xf = x.astype(jnp.float32)
    y = xf * lax.rsqrt(jnp.mean(xf * xf, axis=-1, keepdims=True) + eps)
    return (y * w.astype(jnp.float32)).astype(x.dtype)


def rope_partial(x, pos):
    half = ROT_DIM // 2
    inv_freq = 1.0 / (ROPE_THETA ** (jnp.arange(half, dtype=jnp.float32) * 2.0 / ROT_DIM))
    ang = pos.astype(jnp.float32)[:, None] * inv_freq[None, :]
    cos = jnp.cos(ang)[None, :, None, None, :]
    sin = jnp.sin(ang)[None, :, None, None, :]
    x1 = x[..., :half].astype(jnp.float32)
    x2 = x[..., half:ROT_DIM].astype(jnp.float32)
    rot = jnp.concatenate([x1 * cos - x2 * sin, x2 * cos + x1 * sin], axis=-1).astype(x.dtype)
    return jnp.concatenate([rot, x[..., ROT_DIM:]], axis=-1)


def diff_attn_prompt(q, k, v, lam):
    B, L = q.shape[0], q.shape[1]
    n_blk = L // QBLK
    scale = A_HEAD_DIM ** -0.5
    kpos = jnp.arange(L)

    def blk(i):
        qb = lax.dynamic_slice_in_dim(q, i * QBLK, QBLK, axis=1)
        s = jnp.einsum("bqhcd,bkhcd->bhcqk", qb, k).astype(jnp.float32) * scale
        qpos = i * QBLK + jnp.arange(QBLK)
        s = jnp.where(kpos[None, :] <= qpos[:, None], s, -jnp.inf)
        p = jax.nn.softmax(s, axis=-1)
        pd = p[:, :, 0] - lam * p[:, :, 1]
        return jnp.einsum("bhqk,bkhe->bqhe", pd.astype(v.dtype), v)

    o = lax.map(blk, jnp.arange(n_blk))
    return jnp.moveaxis(o, 0, 1).reshape(B, L, A_HEADS, 2 * A_HEAD_DIM)


def diff_attn_sample(q, k, v, lam, k_past, v_past):
    T = q.shape[1]
    P = k_past.shape[1]
    scale = A_HEAD_DIM ** -0.5
    s_past = jnp.einsum("bqhcd,bkhcd->bhcqk", q, k_past).astype(jnp.float32) * scale
    s_new = jnp.einsum("bqhcd,bkhcd->bhcqk", q, k).astype(jnp.float32) * scale
    s_new = jnp.where(jnp.tril(jnp.ones((T, T), bool)), s_new, -jnp.inf)
    p = jax.nn.softmax(jnp.concatenate([s_past, s_new], axis=-1), axis=-1)
    pd = (p[:, :, 0] - lam * p[:, :, 1]).astype(v.dtype)
    return (jnp.einsum("bhqk,bkhe->bqhe", pd[..., :P], v_past)
            + jnp.einsum("bhqk,bkhe->bqhe", pd[..., P:], v))


def gla_chunk_scan(q, k, v, logf, s0):
    B, L, H, DK = q.shape
    DV = v.shape[-1]
    C = math.gcd(L, CHUNK)
    n = L // C
    tri = jnp.tril(jnp.ones((C, C), bool))

    def to_chunks(t):
        return jnp.moveaxis(t.reshape(B, n, C, H, t.shape[-1]), 1, 0)

    def step(S, inp):
        qc, kc, vc, gc = inp
        b = jnp.cumsum(gc, axis=1)
        o_inter = jnp.einsum("bthk,bhkv->bthv", qc * jnp.exp(b), S)
        diff = jnp.where(tri[None, :, :, None, None], b[:, :, None] - b[:, None, :], -jnp.inf)
        A = jnp.einsum("bthk,btshk->btsh", qc, jnp.exp(diff) * kc[:, None])
        o_intra = jnp.einsum("btsh,bshv->bthv", A, vc)
        b_last = b[:, -1]
        S = S * jnp.exp(b_last)[..., None] + jnp.einsum(
            "bshk,bshv->bhkv", kc * jnp.exp(b_last[:, None] - b), vc)
        return S, o_inter + o_intra

    S, o = lax.scan(step, s0, (to_chunks(q), to_chunks(k), to_chunks(v), to_chunks(logf)))
    return jnp.moveaxis(o, 0, 1).reshape(B, L, H, DV), S


def trunk_layer(x, pos, s0, attend, attn_norm, w_in, w_out, subln_w, lam, lam_init, lb,
                hgrn_norm, ffn_norm, w_gate, w_up, w_down):
    B, L, _ = x.shape
    h = rmsnorm(x, attn_norm)
    z = h @ w_in
    qa, ka, va, qh, fh, ih, gh = jnp.split(z, SPLITS, axis=-1)
    qa = rope_partial(qa.reshape(B, L, A_HEADS, 2, A_HEAD_DIM), pos)
    ka = rope_partial(ka.reshape(B, L, A_HEADS, 2, A_HEAD_DIM), pos)
    va = va.reshape(B, L, A_HEADS, 2 * A_HEAD_DIM)
    oa = attend(qa, ka, va, lam)
    oa = (rmsnorm(oa, subln_w, SUBLN_EPS).astype(jnp.float32) * (1.0 - lam_init)).reshape(B, L, MIX_A)
    f = lb + (1.0 - lb) * jax.nn.sigmoid(fh.astype(jnp.float32))
    heads = lambda t: t.reshape(B, L, H_HEADS, H_EXPAND)
    oh, s_new = gla_chunk_scan(heads(jax.nn.silu(qh.astype(jnp.float32))), heads(1.0 - f),
                               heads(ih.astype(jnp.float32)), heads(jnp.log(f)), s0)
    oh = rmsnorm(oh.reshape(B, L, MIX_H), hgrn_norm) * jax.nn.silu(gh.astype(jnp.float32))
    mix = jnp.concatenate([oa.astype(x.dtype), oh.astype(x.dtype)], axis=-1)
    x = x + mix @ w_out
    h = rmsnorm(x, ffn_norm)
    x = x + (jax.nn.silu(h @ w_gate) * (h @ w_up)) @ w_down
    return x, ka, va, s_new


def setup_inputs(seed: int = 0) -> dict:
    key = jax.random.key(seed)
    ks = jax.random.split(key, 24)
    n_pages = PAST_LEN // PAGE_SIZE
    n_used = DEC_BATCH * n_pages
    n_pool = n_used + n_used // 4
    nrm = lambda k, shape, s: jax.random.normal(k, shape, jnp.float32) * s
    gain = lambda k, shape: 1.0 + 0.01 * jax.random.normal(k, shape, jnp.float32)
    page_table = jax.random.permutation(ks[5], n_pool)[:n_used].reshape(DEC_BATCH, n_pages).astype(jnp.int32)
    return {
        "x_prompt": nrm(ks[0], (BATCH, SEQ, D_MODEL), 1.0),
        "x_sample": nrm(ks[1], (DEC_BATCH, DEC_SEQ, D_MODEL), 1.0),
        "cache_k": nrm(ks[2], (DEPTH, n_pool, PAGE_SIZE, A_HEADS, 2, A_HEAD_DIM), 1.0),
        "cache_v": nrm(ks[3], (DEPTH, n_pool, PAGE_SIZE, A_HEADS, 2 * A_HEAD_DIM), 1.0),
        "state_hgrn": nrm(ks[4], (DEPTH, DEC_BATCH, H_HEADS, H_EXPAND, H_EXPAND), 0.3),
        "page_table": page_table,
        "attn_norm": gain(ks[6], (DEPTH, D_MODEL)),
        "w_in": nrm(ks[7], (DEPTH, D_MODEL, N_IN), D_MODEL ** -0.5),
        "w_out": nrm(ks[8], (DEPTH, MIX_A + MIX_H, D_MODEL), (MIX_A + MIX_H) ** -0.5),
        "lambda_q1": nrm(ks[9], (DEPTH, A_HEAD_DIM), 0.1),
        "lambda_k1": nrm(ks[10], (DEPTH, A_HEAD_DIM), 0.1),
        "lambda_q2": nrm(ks[11], (DEPTH, A_HEAD_DIM), 0.1),
        "lambda_k2": nrm(ks[12], (DEPTH, A_HEAD_DIM), 0.1),
        "subln_w": gain(ks[13], (DEPTH, 2 * A_HEAD_DIM)),
        "hgrn_lb_logits": nrm(ks[14], (DEPTH + 1, MIX_H), 0.5),
        "hgrn_norm": gain(ks[15], (DEPTH, MIX_H)),
        "ffn_norm": gain(ks[16], (DEPTH, D_MODEL)),
        "w_gate": nrm(ks[17], (DEPTH, D_MODEL, D_FF), D_MODEL ** -0.5),
        "w_up": nrm(ks[18], (DEPTH, D_MODEL, D_FF), D_MODEL ** -0.5),
        "w_down": nrm(ks[19], (DEPTH, D_FF, D_MODEL), D_FF ** -0.5),
        "final_norm": gain(ks[20], (D_MODEL,)),
    }


def reference(x_prompt, x_sample, cache_k, cache_v, state_hgrn, page_table, attn_norm, w_in, w_out,
              lambda_q1, lambda_k1, lambda_q2, lambda_k2, subln_w, hgrn_lb_logits, hgrn_norm,
              ffn_norm, w_gate, w_up, w_down, final_norm):
    Bd, n_pages = page_table.shape
    past_len = n_pages * cache_k.shape[2]
    pos_p = jnp.arange(x_prompt.shape[1])
    pos_s = past_len + jnp.arange(x_sample.shape[1])
    lb_all = jnp.cumsum(jax.nn.softmax(hgrn_lb_logits.astype(jnp.float32), axis=0), axis=0)
    xp, xs = x_prompt, x_sample
    kp, vp, sp, ks_, vs_, ss = [], [], [], [], [], []
    for l in range(DEPTH):
        lam_init = 0.8 - 0.6 * math.exp(-0.3 * l)
        lam = (jnp.exp(jnp.sum(lambda_q1[l].astype(jnp.float32) * lambda_k1[l].astype(jnp.float32)))
               - jnp.exp(jnp.sum(lambda_q2[l].astype(jnp.float32) * lambda_k2[l].astype(jnp.float32)))
               + lam_init)
        lw = (attn_norm[l], w_in[l], w_out[l], subln_w[l], lam, lam_init, lb_all[l],
              hgrn_norm[l], ffn_norm[l], w_gate[l], w_up[l], w_down[l])
        s0_p = jnp.zeros((xp.shape[0], H_HEADS, H_EXPAND, H_EXPAND), jnp.float32)
        xp, k_new, v_new, s_new = trunk_layer(xp, pos_p, s0_p, diff_attn_prompt, *lw)
        kp.append(k_new); vp.append(v_new); sp.append(s_new)
        k_past = cache_k[l][page_table].reshape(Bd, past_len, A_HEADS, 2, A_HEAD_DIM)
        v_past = cache_v[l][page_table].reshape(Bd, past_len, A_HEADS, 2 * A_HEAD_DIM)
        attend_s = functools.partial(diff_attn_sample, k_past=k_past, v_past=v_past)
        xs, k_new, v_new, s_new = trunk_layer(xs, pos_s, state_hgrn[l].astype(jnp.float32), attend_s, *lw)
        ks_.append(k_new); vs_.append(v_new); ss.append(s_new)
    y_prompt = rmsnorm(xp, final_norm)
    y_sample = rmsnorm(xs, final_norm)
    return (y_prompt, y_sample, jnp.stack(kp), jnp.stack(vp), jnp.stack(sp),
            jnp.stack(ks_), jnp.stack(vs_), jnp.stack(ss))
```

```python
import functools
import math

import jax
import jax.numpy as jnp
from jax import lax
from jax.experimental import pallas as pl
from jax.experimental.pallas import tpu as pltpu

D_MODEL = 1024
MIX_A = 512
MIX_H = 512
A_HEAD_DIM = 64
A_HEADS = 4
ROT_DIM = 16
ROPE_THETA = 500000.0
H_EXPAND = 128
H_HEADS = 4
D_FF = 2816
N_IN = 3 * MIX_A + 4 * MIX_H
EPS = 1e-6
SUBLN_EPS = 1e-5
LOG2E = 1.4426950408889634

LANES = 128
HGRN_CHUNK = 64
HGRN_SUB = 16
FF_CHUNK = 256
VMEM_LIMIT = 56 * 1024 * 1024

F32 = jnp.float32
BF16 = jnp.bfloat16


def _dot(a, b):
    return jnp.dot(a, b, preferred_element_type=F32)


def _dot_nt(a, b):
    return lax.dot_general(a, b, (((1,), (1,)), ((), ())), preferred_element_type=F32)


def _dot_tn(a, b):
    return lax.dot_general(a, b, (((0,), (0,)), ((), ())), preferred_element_type=F32)


def _split3(a):
    hi = a.astype(BF16)
    r = a - hi.astype(F32)
    mid = r.astype(BF16)
    lo = (r - mid.astype(F32)).astype(BF16)
    return hi, mid, lo


def _sigmoid(x):
    return 1.0 / (1.0 + jnp.exp(-x))


def _rms(x, w, eps):
    ms = jnp.mean(x * x, axis=-1, keepdims=True)
    return x * lax.rsqrt(ms + eps) * w


def _lower_bound(lb_logits_ref, layer):
    lg = lb_logits_ref[...]
    e = jnp.exp(lg - jnp.max(lg, axis=0, keepdims=True))
    return jnp.sum(e[: layer + 1], axis=0, keepdims=True) / jnp.sum(e, axis=0, keepdims=True)


def _lambda(lq1_ref, lk1_ref, lq2_ref, lk2_ref, lam_init):
    s1 = jnp.sum(lq1_ref[...] * lk1_ref[...], axis=-1, keepdims=True)
    s2 = jnp.sum(lq2_ref[...] * lk2_ref[...], axis=-1, keepdims=True)
    return jnp.exp(s1) - jnp.exp(s2) + lam_init


def _inproj_kernel(x_ref, nw_ref, w_ref, c_ref, s1_ref, s2_ref,
                   q_ref, k_ref, v_ref, kb_ref, vb_ref, zh_ref):
    h = _rms(x_ref[...], nw_ref[...], EPS).astype(BF16)
    cs, s1, s2 = c_ref[...], s1_ref[...], s2_ref[...]

    def proj(a, b):
        return _dot(h, w_ref[:, a:b])

    def rope(z):
        outs = []
        for j in range(MIX_A // LANES):
            zj = z[:, j * LANES:(j + 1) * LANES]
            outs.append(zj * cs + pltpu.roll(zj, LANES - ROT_DIM // 2, 1) * s1
                        + pltpu.roll(zj, ROT_DIM // 2, 1) * s2)
        return jnp.concatenate(outs, axis=1)

    q_ref[...] = (rope(proj(0, MIX_A)) * (A_HEAD_DIM ** -0.5 * LOG2E)).astype(BF16)
    k = rope(proj(MIX_A, 2 * MIX_A))
    k_ref[...] = k
    kb_ref[...] = k.astype(BF16)
    v = proj(2 * MIX_A, 3 * MIX_A)
    v_ref[...] = v
    vb_ref[...] = v.astype(BF16)
    zh_ref[...] = proj(3 * MIX_A, N_IN)


def _inproj(x, norm_w, w_in_bf, cs, s1, s2, tm, n_pos_blocks):
    n = x.shape[0]
    grid = (n // tm,)
    row = lambda i: (i, 0)
    pos = lambda i: (i % n_pos_blocks, 0)
    const = lambda i: (0, 0)
    return pl.pallas_call(
        _inproj_kernel,
        grid=grid,
        in_specs=[
            pl.BlockSpec((tm, D_MODEL), row),
            pl.BlockSpec((1, D_MODEL), const),
            pl.BlockSpec((D_MODEL, N_IN), const),
            pl.BlockSpec((tm, LANES), pos),
            pl.BlockSpec((tm, LANES), pos),
            pl.BlockSpec((tm, LANES), pos),
        ],
        out_specs=[
            pl.BlockSpec((tm, MIX_A), row),
            pl.BlockSpec((tm, MIX_A), row),
            pl.BlockSpec((tm, MIX_A), row),
            pl.BlockSpec((tm, MIX_A), row),
            pl.BlockSpec((tm, MIX_A), row),
            pl.BlockSpec((tm, 4 * MIX_H), row),
        ],
        out_shape=[
            jax.ShapeDtypeStruct((n, MIX_A), BF16),
            jax.ShapeDtypeStruct((n, MIX_A), F32),
            jax.ShapeDtypeStruct((n, MIX_A), F32),
            jax.ShapeDtypeStruct((n, MIX_A), BF16),
            jax.ShapeDtypeStruct((n, MIX_A), BF16),
            jax.ShapeDtypeStruct((n, 4 * MIX_H), F32),
        ],
        compiler_params=pltpu.CompilerParams(
            dimension_semantics=("arbitrary",), vmem_limit_bytes=VMEM_LIMIT),
        name="inproj",
    )(x, norm_w, w_in_bf, cs, s1, s2)


def _rope_tables(pos):
    half = ROT_DIM // 2
    inv_freq = 1.0 / (ROPE_THETA ** (jnp.arange(half, dtype=F32) * 2.0 / ROT_DIM))
    ang = pos.astype(F32)[:, None] * inv_freq[None, :]
    cos, sin = jnp.cos(ang), jnp.sin(ang)
    n = pos.shape[0]
    ones = jnp.ones((n, A_HEAD_DIM - ROT_DIM), F32)
    zeros = jnp.zeros((n, A_HEAD_DIM - ROT_DIM), F32)
    zh = jnp.zeros((n, half), F32)
    cs = jnp.concatenate([cos, cos, ones], axis=1)
    s1 = jnp.concatenate([-sin, zh, zeros], axis=1)
    s2 = jnp.concatenate([zh, sin, zeros], axis=1)
    tile2 = lambda t: jnp.concatenate([t, t], axis=1)
    return tile2(cs), tile2(s1), tile2(s2)


def _hgrn_masks():
    t = lax.broadcasted_iota(jnp.int32, (HGRN_CHUNK, HGRN_CHUNK), 0)
    s = lax.broadcasted_iota(jnp.int32, (HGRN_CHUNK, HGRN_CHUNK), 1)
    ts, ss = t // HGRN_SUB, s // HGRN_SUB
    m0 = jnp.where(ts == ss, jnp.where(s <= t, 1, 0), 0)
    m1 = jnp.where(ts == ss + 1, jnp.where(ts // 2 == ss // 2, 1, 0), 0)
    m2 = jnp.where(ts >= 2, jnp.where(ss < 2, 1, 0), 0)
    return m0, m1, m2


def _hgrn_prompt_kernel(layer, qh_ref, fh_ref, ih_ref, gh_ref, lbl_ref, nw_ref,
                        o_ref, s_ref, st_ref):
    tm = qh_ref.shape[0]
    j = pl.program_id(1)

    @pl.when(j == 0)
    def _():
        st_ref[...] = jnp.zeros_like(st_ref)

    lb = _lower_bound(lbl_ref, layer)
    nw = nw_ref[...]
    m0, m1, m2 = _hgrn_masks()
    rows = lax.broadcasted_iota(jnp.int32, (HGRN_CHUNK, 1), 0)
    tr = lax.broadcasted_iota(jnp.int32, (HGRN_CHUNK, HGRN_CHUNK), 0)
    tc = lax.broadcasted_iota(jnp.int32, (HGRN_CHUNK, HGRN_CHUNK), 1)
    tri = jnp.where(tc <= tr, 1.0, 0.0).astype(BF16)

    def chunk(c, carry):
        r0 = pl.multiple_of(c * HGRN_CHUNK, HGRN_CHUNK)
        sl = pl.ds(r0, HGRN_CHUNK)
        qh, fh, v, gh = qh_ref[sl, :], fh_ref[sl, :], ih_ref[sl, :], gh_ref[sl, :]
        f = lb + (1.0 - lb) * _sigmoid(fh)
        g = jnp.log(f)
        q = qh * _sigmoid(qh)
        k = 1.0 - f
        ghi, gmid, glo = _split3(g)
        bc = _dot(tri, glo) + _dot(tri, gmid) + _dot(tri, ghi)
        r1 = bc[HGRN_SUB - 1:HGRN_SUB, :]
        r2 = bc[2 * HGRN_SUB - 1:2 * HGRN_SUB, :]
        r3 = bc[3 * HGRN_SUB - 1:3 * HGRN_SUB, :]
        blast = bc[HGRN_CHUNK - 1:HGRN_CHUNK, :]
        ref0 = jnp.where(rows < HGRN_SUB, 0.0,
                         jnp.where(rows < 2 * HGRN_SUB, r1, jnp.where(rows < 3 * HGRN_SUB, r2, r3)))
        ref1 = jnp.where(rows < 2 * HGRN_SUB, r1, r3)
        qs, ks = [], []
        for ref in (ref0, ref1, r2):
            d = bc - ref
            qs.append((q * jnp.exp(d)).astype(BF16))
            ks.append((k * jnp.exp(-d)).astype(BF16))
        q_in = (q * jnp.exp(bc)).astype(BF16)
        k_out = (k * jnp.exp(blast - bc)).astype(BF16)
        dec = jnp.exp(blast)
        vb = v.astype(BF16)
        outs = []
        for hh in range(H_HEADS):
            hs = slice(hh * H_EXPAND, (hh + 1) * H_EXPAND)
            a0 = _dot_nt(qs[0][:, hs], ks[0][:, hs])
            a1 = _dot_nt(qs[1][:, hs], ks[1][:, hs])
            a2 = _dot_nt(qs[2][:, hs], ks[2][:, hs])
            a = jnp.where(m0 == 1, a0, jnp.where(m1 == 1, a1, jnp.where(m2 == 1, a2, 0.0)))
            st = st_ref[hh]
            o = _dot_nt(q_in[:, hs], st.astype(BF16)) + _dot(a.astype(BF16), vb[:, hs])
            st_ref[hh] = st * dec[:, hs] + _dot_tn(vb[:, hs], k_out[:, hs])
            outs.append(o)
        oh = jnp.concatenate(outs, axis=1)
        o_ref[sl, :] = (_rms(oh, nw, EPS) * (gh * _sigmoid(gh))).astype(BF16)
        return carry

    lax.fori_loop(0, tm // HGRN_CHUNK, chunk, 0)

    @pl.when(j == pl.num_programs(1) - 1)
    def _():
        for hh in range(H_HEADS):
            s_ref[0, hh] = st_ref[hh].T


def _hgrn_prompt(zh, lb_logits, norm_w, layer, batch, seq, tm):
    n = zh.shape[0]
    nb = seq // tm
    col = lambda cidx: (lambda b, j: (b * nb + j, cidx))
    const = lambda b, j: (0, 0)
    return pl.pallas_call(
        functools.partial(_hgrn_prompt_kernel, layer),
        grid=(batch, nb),
        in_specs=[pl.BlockSpec((tm, MIX_H), col(c)) for c in range(4)] + [
            pl.BlockSpec(lb_logits.shape, const),
            pl.BlockSpec((1, MIX_H), const),
        ],
        out_specs=[
            pl.BlockSpec((tm, MIX_H), lambda b, j: (b * nb + j, 0)),
            pl.BlockSpec((1, H_HEADS, H_EXPAND, H_EXPAND), lambda b, j: (b, 0, 0, 0)),
        ],
        out_shape=[
            jax.ShapeDtypeStruct((n, MIX_H), BF16),
            jax.ShapeDtypeStruct((batch, H_HEADS, H_EXPAND, H_EXPAND), F32),
        ],
        scratch_shapes=[pltpu.VMEM((H_HEADS, H_EXPAND, H_EXPAND), F32)],
        compiler_params=pltpu.CompilerParams(
            dimension_semantics=("arbitrary", "arbitrary"), vmem_limit_bytes=VMEM_LIMIT),
        name="hgrn_prompt",
    )(zh, zh, zh, zh, lb_logits, norm_w)


def _hgrn_sample_kernel(layer, t_len, zh_ref, s0_ref, lbl_ref, nw_ref, o_ref, s_ref):
    rows_n = zh_ref.shape[0]
    bb = rows_n // t_len
    lb = _lower_bound(lbl_ref, layer)
    zh = zh_ref[...]
    qh, fh = zh[:, 0:MIX_H], zh[:, MIX_H:2 * MIX_H]
    v, gh = zh[:, 2 * MIX_H:3 * MIX_H], zh[:, 3 * MIX_H:4 * MIX_H]
    f = lb + (1.0 - lb) * _sigmoid(fh)
    g = jnp.log(f)
    q = qh * _sigmoid(qh)
    k = 1.0 - f
    tr = lax.broadcasted_iota(jnp.int32, (rows_n, rows_n), 0)
    tc = lax.broadcasted_iota(jnp.int32, (rows_n, rows_n), 1)
    same = tr // t_len == tc // t_len
    causal = jnp.where(same, jnp.where(tc <= tr, 1, 0), 0)
    tri = causal.astype(F32).astype(BF16)
    last = jnp.where(same, 1.0, 0.0).astype(BF16)
    ghi, gmid, glo = _split3(g)
    bc = _dot(tri, glo) + _dot(tri, gmid) + _dot(tri, ghi)
    blast = _dot(last, glo) + _dot(last, gmid) + _dot(last, ghi)
    q_in = (q * jnp.exp(bc)).astype(BF16)
    k_in = (k * jnp.exp(-bc)).astype(BF16)
    k_out = k * jnp.exp(blast - bc)
    vb = v.astype(BF16)
    rowid = lax.broadcasted_iota(jnp.int32, (rows_n, 1), 0)
    ones = jnp.ones((rows_n, LANES), BF16)
    outs = []
    for hh in range(H_HEADS):
        hs = slice(hh * H_EXPAND, (hh + 1) * H_EXPAND)
        a = jnp.where(causal == 1, _dot_nt(q_in[:, hs], k_in[:, hs]), 0.0)
        o = _dot(a.astype(BF16), vb[:, hs])
        for b in range(bb):
            mine = rowid // t_len == b
            s0 = s0_ref[b, hh]
            o = o + jnp.where(mine, _dot(q_in[:, hs], s0.astype(BF16)), 0.0)
            kb = jnp.where(mine, k_out[:, hs], 0.0).astype(BF16)
            dcol = (_dot_tn(jnp.where(mine, glo[:, hs], 0), ones)
                    + _dot_tn(jnp.where(mine, gmid[:, hs], 0), ones)
                    + _dot_tn(jnp.where(mine, ghi[:, hs], 0), ones))
            s_ref[b, hh] = s0 * jnp.exp(dcol) + _dot_tn(kb, vb[:, hs])
        outs.append(o)
    oh = jnp.concatenate(outs, axis=1)
    o_ref[...] = (_rms(oh, nw_ref[...], EPS) * (gh * _sigmoid(gh))).astype(BF16)


def _hgrn_sample(zh, s0, lb_logits, norm_w, layer, t_len, bb):
    n = zh.shape[0]
    nseq = n // t_len
    const = lambda i: (0, 0)
    return pl.pallas_call(
        functools.partial(_hgrn_sample_kernel, layer, t_len),
        grid=(nseq // bb,),
        in_specs=[
            pl.BlockSpec((bb * t_len, 4 * MIX_H), lambda i: (i, 0)),
            pl.BlockSpec((bb, H_HEADS, H_EXPAND, H_EXPAND), lambda i: (i, 0, 0, 0)),
            pl.BlockSpec(lb_logits.shape, const),
            pl.BlockSpec((1, MIX_H), const),
        ],
        out_specs=[
            pl.BlockSpec((bb * t_len, MIX_H), lambda i: (i, 0)),
            pl.BlockSpec((bb, H_HEADS, H_EXPAND, H_EXPAND), lambda i: (i, 0, 0, 0)),
        ],
        out_shape=[
            jax.ShapeDtypeStruct((n, MIX_H), BF16),
            jax.ShapeDtypeStruct(s0.shape, F32),
        ],
        compiler_params=pltpu.CompilerParams(
            dimension_semantics=("arbitrary",), vmem_limit_bytes=VMEM_LIMIT),
        name="hgrn_sample",
    )(zh, s0, lb_logits, norm_w)


def _attn_prompt_kernel(lam_init, tq, q_ref, k_ref, v_ref, lq1, lk1, lq2, lk2, sw_ref,
                        o_ref, m_ref, l_ref, acc_ref):
    qi = pl.program_id(2)
    q = q_ref[...]
    lane = lax.broadcasted_iota(jnp.int32, (1, LANES), 1)
    m_ref[...] = jnp.full_like(m_ref, -jnp.inf)
    l_ref[...] = jnp.zeros_like(l_ref)
    acc_ref[...] = jnp.zeros_like(acc_ref)

    def step(ki, masked):
        r0 = pl.multiple_of(ki * tq, tq)
        k = k_ref[pl.ds(r0, tq), :]
        v = v_ref[pl.ds(r0, tq), :]
        for c in range(2):
            kc = jnp.where((lane // A_HEAD_DIM) == c, k, jnp.zeros_like(k))
            s = _dot_nt(q, kc)
            if masked:
                rr = lax.broadcasted_iota(jnp.int32, (tq, tq), 0)
                cc = lax.broadcasted_iota(jnp.int32, (tq, tq), 1)
                s = jnp.where(cc <= rr, s, -jnp.inf)
            m_prev = m_ref[c]
            m_next = jnp.maximum(m_prev, jnp.max(s, axis=1, keepdims=True))
            alpha = jnp.exp2(m_prev - m_next)
            p = jnp.exp2(s - pltpu.repeat(m_next, tq // LANES, axis=1))
            l_ref[c] = alpha * l_ref[c] + jnp.sum(p, axis=1, keepdims=True)
            acc_ref[c] = acc_ref[c] * alpha + _dot(p.astype(BF16), v)
            m_ref[c] = m_next

    def body(ki, carry):
        step(ki, False)
        return carry

    lax.fori_loop(0, qi, body, 0)
    step(qi, True)

    lam = _lambda(lq1, lk1, lq2, lk2, lam_init)
    o = acc_ref[0] / l_ref[0] - lam * (acc_ref[1] / l_ref[1])
    o_ref[...] = (_rms(o, sw_ref[...], SUBLN_EPS) * (1.0 - lam_init)).astype(BF16)


def _attn_prompt(q, kb, vb, lams, subln_w, lam_init, batch, seq, tq):
    n = q.shape[0]
    nq = seq // tq
    kb3 = kb.reshape(batch, seq, MIX_A)
    vb3 = vb.reshape(batch, seq, MIX_A)
    const = lambda b, h, i: (0, 0)
    return pl.pallas_call(
        functools.partial(_attn_prompt_kernel, lam_init, tq),
        grid=(batch, A_HEADS, nq),
        in_specs=[
            pl.BlockSpec((tq, LANES), lambda b, h, i: (b * nq + i, h)),
            pl.BlockSpec((None, seq, LANES), lambda b, h, i: (b, 0, h)),
            pl.BlockSpec((None, seq, LANES), lambda b, h, i: (b, 0, h)),
        ] + [pl.BlockSpec((1, A_HEAD_DIM), const)] * 4 + [
            pl.BlockSpec((1, 2 * A_HEAD_DIM), const),
        ],
        out_specs=pl.BlockSpec((tq, LANES), lambda b, h, i: (b * nq + i, h)),
        out_shape=jax.ShapeDtypeStruct((n, MIX_A), BF16),
        scratch_shapes=[
            pltpu.VMEM((2, tq, LANES), F32),
            pltpu.VMEM((2, tq, LANES), F32),
            pltpu.VMEM((2, tq, LANES), F32),
        ],
        compiler_params=pltpu.CompilerParams(
            dimension_semantics=("arbitrary", "arbitrary", "arbitrary"),
            vmem_limit_bytes=VMEM_LIMIT),
        name="attn_prompt",
    )(q, kb3, vb3, *lams, subln_w)


def _attn_sample_kernel(lam_init, n_pages, t_len, pt_ref, q_ref, kn_ref, vn_ref, *rest):
    kp_refs = rest[:n_pages]
    vp_refs = rest[n_pages:2 * n_pages]
    lq1, lk1, lq2, lk2, sw_ref, o_ref = rest[2 * n_pages:]
    nrow = 2 * A_HEADS * t_len
    r = lax.broadcasted_iota(jnp.int32, (nrow, 1), 0)
    r_tok = r % t_len
    r_grp = (r // t_len) % A_HEADS * 2 + r // (A_HEADS * t_len)
    lane = lax.broadcasted_iota(jnp.int32, (1, MIX_A), 1)
    q4 = q_ref[...].astype(F32)
    qt = jnp.zeros((nrow, MIX_A), F32)
    for t in range(t_len):
        qt = jnp.where(r_tok == t, q4[t:t + 1, :], qt)
    qt = jnp.where(lane // A_HEAD_DIM == r_grp, qt, 0.0)
    qtb = qt.astype(BF16)

    s_past = [_dot_nt(qtb, kp[...].astype(BF16)) for kp in kp_refs]
    kn = kn_ref[...]
    s_new = []
    for t in range(t_len):
        sn = jnp.sum(qt * kn[t:t + 1, :], axis=-1, keepdims=True)
        s_new.append(jnp.where(r_tok >= t, sn, -jnp.inf))
    m = s_new[0]
    for sp in s_past:
        m = jnp.maximum(m, jnp.max(sp, axis=-1, keepdims=True))
    for sn in s_new[1:]:
        m = jnp.maximum(m, sn)
    p_past = [jnp.exp2(sp - m) for sp in s_past]
    p_new = [jnp.exp2(sn - m) for sn in s_new]
    l = p_new[0]
    for pn in p_new[1:]:
        l = l + pn
    for pp in p_past:
        l = l + jnp.sum(pp, axis=-1, keepdims=True)
    inv = 1.0 / l
    lam = _lambda(lq1, lk1, lq2, lk2, lam_init)
    half = nrow // 2
    w = jnp.where(r < half, inv, -lam * inv)

    def combine(p):
        pw = p * w
        return pw[:half] + pw[half:]

    res = jnp.zeros((half, MIX_A), F32)
    for pp, vp in zip(p_past, vp_refs):
        res = res + _dot(combine(pp).astype(BF16), vp[...].astype(BF16))
    vn = vn_ref[...]
    for t in range(t_len):
        res = res + combine(p_new[t]) * vn[t:t + 1, :]
    rh = lax.broadcasted_iota(jnp.int32, (half, 1), 0) // t_len
    res = jnp.where(lane // (2 * A_HEAD_DIM) == rh, res, 0.0)
    ms = jnp.sum(res * res, axis=-1, keepdims=True) * (1.0 / (2 * A_HEAD_DIM))
    sw = jnp.concatenate([sw_ref[...]] * A_HEADS, axis=1)
    y = res * lax.rsqrt(ms + SUBLN_EPS) * sw * (1.0 - lam_init)
    out = y[0:t_len]
    for hh in range(1, A_HEADS):
        out = out + y[hh * t_len:(hh + 1) * t_len]
    o_ref[...] = out.astype(BF16)


def _attn_sample(q, k_new, v_new, cache_k, cache_v, page_table, lams, subln_w, lam_init, t_len):
    nseq, n_pages = page_table.shape
    n_pool, page = cache_k.shape[0], cache_k.shape[1]
    ck = cache_k.reshape(n_pool, page, MIX_A)
    cv = cache_v.reshape(n_pool, page, MIX_A)
    q3 = q.reshape(nseq, t_len, MIX_A)
    k3 = k_new.reshape(nseq, t_len, MIX_A)
    v3 = v_new.reshape(nseq, t_len, MIX_A)
    tok = pl.BlockSpec((None, t_len, MIX_A), lambda b, pt: (b, 0, 0))
    const = lambda b, pt: (0, 0)
    page_spec = lambda j: pl.BlockSpec((None, page, MIX_A), lambda b, pt: (pt[b, j], 0, 0))
    grid_spec = pltpu.PrefetchScalarGridSpec(
        num_scalar_prefetch=1,
        grid=(nseq,),
        in_specs=[tok, tok, tok]
        + [page_spec(j) for j in range(n_pages)]
        + [page_spec(j) for j in range(n_pages)]
        + [pl.BlockSpec((1, A_HEAD_DIM), const)] * 4
        + [pl.BlockSpec((1, 2 * A_HEAD_DIM), const)],
        out_specs=tok,
    )
    out = pl.pallas_call(
        functools.partial(_attn_sample_kernel, lam_init, n_pages, t_len),
        grid_spec=grid_spec,
        out_shape=jax.ShapeDtypeStruct((nseq, t_len, MIX_A), BF16),
        compiler_params=pltpu.CompilerParams(
            dimension_semantics=("arbitrary",), vmem_limit_bytes=VMEM_LIMIT),
        name="attn_sample",
    )(page_table, q3, k3, v3, *([ck] * n_pages), *([cv] * n_pages), *lams, subln_w)
    return out.reshape(nseq * t_len, MIX_A)


def _out_ffn_kernel(x_ref, ma_ref, mh_ref, wo_ref, fnw_ref, wg_ref, wu_ref, wd_ref, finw_ref, y_ref):
    x2 = x_ref[...] + _dot(ma_ref[...], wo_ref[0:MIX_A, :]) + _dot(mh_ref[...], wo_ref[MIX_A:, :])
    h2 = _rms(x2, fnw_ref[...], EPS).astype(BF16)
    acc = jnp.zeros_like(x2)
    for c in range(D_FF // FF_CHUNK):
        cs = slice(c * FF_CHUNK, (c + 1) * FF_CHUNK)
        g = _dot(h2, wg_ref[:, cs])
        u = _dot(h2, wu_ref[:, cs])
        acc = acc + _dot((g * _sigmoid(g) * u).astype(BF16), wd_ref[cs, :])
    y_ref[...] = _rms(x2 + acc, finw_ref[...], EPS)


def _out_ffn(x, mix_a, mix_h, wo, fnw, wg, wu, wd, finw, tm):
    n = x.shape[0]
    row = lambda i: (i, 0)
    const = lambda i: (0, 0)
    return pl.pallas_call(
        _out_ffn_kernel,
        grid=(n // tm,),
        in_specs=[
            pl.BlockSpec((tm, D_MODEL), row),
            pl.BlockSpec((tm, MIX_A), row),
            pl.BlockSpec((tm, MIX_H), row),
            pl.BlockSpec((MIX_A + MIX_H, D_MODEL), const),
            pl.BlockSpec((1, D_MODEL), const),
            pl.BlockSpec((D_MODEL, D_FF), const),
            pl.BlockSpec((D_MODEL, D_FF), const),
            pl.BlockSpec((D_FF, D_MODEL), const),
            pl.BlockSpec((1, D_MODEL), const),
        ],
        out_specs=pl.BlockSpec((tm, D_MODEL), row),
        out_shape=jax.ShapeDtypeStruct((n, D_MODEL), F32),
        compiler_params=pltpu.CompilerParams(
            dimension_semantics=("arbitrary",), vmem_limit_bytes=VMEM_LIMIT),
        name="out_ffn",
    )(x, mix_a, mix_h, wo, fnw, wg, wu, wd, finw)


def _pick(n, prefs):
    for p in prefs:
        if n % p == 0:
            return p
    return n


def kernel(x_prompt, x_sample, cache_k, cache_v, state_hgrn, page_table, attn_norm, w_in, w_out,
           lambda_q1, lambda_k1, lambda_q2, lambda_k2, subln_w, hgrn_lb_logits, hgrn_norm,
           ffn_norm, w_gate, w_up, w_down, final_norm):
    batch, seq, _ = x_prompt.shape
    nseq, t_len, _ = x_sample.shape
    n_pages = page_table.shape[1]
    page = cache_k.shape[2]
    depth = w_in.shape[0]
    assert depth == 1, "single-layer step"
    past_len = n_pages * page
    layer = 0
    lam_init = 0.8 - 0.6 * math.exp(-0.3 * layer)

    row2 = lambda a: a.reshape(1, -1)
    w_in_bf = w_in[layer].astype(BF16)
    wo = w_out[layer].astype(BF16)
    wg = w_gate[layer].astype(BF16)
    wu = w_up[layer].astype(BF16)
    wd = w_down[layer].astype(BF16)
    lams = [row2(a[layer]) for a in (lambda_q1, lambda_k1, lambda_q2, lambda_k2)]
    sw = row2(subln_w[layer])
    an, hn, fn, finw = row2(attn_norm[layer]), row2(hgrn_norm[layer]), row2(ffn_norm[layer]), row2(final_norm)

    n_p = batch * seq
    xp = x_prompt.reshape(n_p, D_MODEL)
    tm_p = _pick(seq, (256, 128, 64))
    cs, s1, s2 = _rope_tables(jnp.arange(seq))
    q, k_new, v_new, kb, vb, zh = _inproj(xp, an, w_in_bf, cs, s1, s2, tm_p, seq // tm_p)
    mix_h, s_prompt = _hgrn_prompt(zh, hgrn_lb_logits, hn, layer, batch, seq, _pick(seq, (512, 256, 128, 64)))
    mix_a = _attn_prompt(q, kb, vb, lams, sw, lam_init, batch, seq, _pick(seq, (512, 256, 128)))
    y_prompt = _out_ffn(xp, mix_a, mix_h, wo, fn, wg, wu, wd, finw, tm_p).reshape(batch, seq, D_MODEL)

    n_s = nseq * t_len
    xs = x_sample.reshape(n_s, D_MODEL)
    tm_s = _pick(n_s, (256, 128, 64, 32, 16, 8))
    cs, s1, s2 = _rope_tables(past_len + jnp.arange(n_s) % t_len)
    qs, ks_new, vs_new, _, _, zhs = _inproj(xs, an, w_in_bf, cs, s1, s2, tm_s, n_s // tm_s)
    bb = _pick(nseq, (8, 4, 2, 1))
    mix_h_s, s_sample = _hgrn_sample(zhs, state_hgrn[layer].astype(F32), hgrn_lb_logits, hn, layer, t_len, bb)
    mix_a_s = _attn_sample(qs, ks_new, vs_new, cache_k[layer], cache_v[layer], page_table, lams, sw,
                           lam_init, t_len)
    y_sample = _out_ffn(xs, mix_a_s, mix_h_s, wo, fn, wg, wu, wd, finw, tm_s).reshape(nseq, t_len, D_MODEL)

    return (y_prompt, y_sample,
            k_new.reshape(1, batch, seq, A_HEADS, 2, A_HEAD_DIM),
            v_new.reshape(1, batch, seq, A_HEADS, 2 * A_HEAD_DIM),
            s_prompt[None],
            ks_new.reshape(1, nseq, t_len, A_HEADS, 2, A_HEAD_DIM),
            vs_new.reshape(1, nseq, t_len, A_HEADS, 2 * A_HEAD_DIM),
            s_sample[None])
```

```python
import functools
import math

import jax
import jax.numpy as jnp
from jax import lax
from jax.experimental import pallas as pl
from jax.experimental.pallas import tpu as pltpu

D_MODEL = 1024
MIX_A = 512
MIX_H = 512
A_HEAD_DIM = 64
A_HEADS = 4
ROT_DIM = 16
ROPE_THETA = 500000.0
H_EXPAND = 128
H_HEADS = 4
D_FF = 2816
N_IN = 3 * MIX_A + 4 * MIX_H
EPS = 1e-6
SUBLN_EPS = 1e-5
LOG2E = 1.4426950408889634

LANES = 128
HGRN_CHUNK = 64
HGRN_SUB = 16
FF_CHUNK = 256
VMEM_LIMIT = 56 * 1024 * 1024

F32 = jnp.float32
BF16 = jnp.bfloat16


def _dot(a, b):
    return jnp.dot(a, b, preferred_element_type=F32)


def _dot_nt(a, b):
    return lax.dot_general(a, b, (((1,), (1,)), ((), ())), preferred_element_type=F32)


def _dot_tn(a, b):
    return lax.dot_general(a, b, (((0,), (0,)), ((), ())), preferred_element_type=F32)


def _split3(a):
    hi = a.astype(BF16)
    r = a - hi.astype(F32)
    mid = r.astype(BF16)
    lo = (r - mid.astype(F32)).astype(BF16)
    return hi, mid, lo


def _sigmoid(x):
    return 1.0 / (1.0 + jnp.exp(-x))


def _rms(x, w, eps):
    ms = jnp.mean(x * x, axis=-1, keepdims=True)
    return x * lax.rsqrt(ms + eps) * w


def _lower_bound(lb_logits_ref, layer):
    lg = lb_logits_ref[...]
    e = jnp.exp(lg - jnp.max(lg, axis=0, keepdims=True))
    return jnp.sum(e[: layer + 1], axis=0, keepdims=True) / jnp.sum(e, axis=0, keepdims=True)


def _lambda(lq1_ref, lk1_ref, lq2_ref, lk2_ref, lam_init):
    s1 = jnp.sum(lq1_ref[...] * lk1_ref[...], axis=-1, keepdims=True)
    s2 = jnp.sum(lq2_ref[...] * lk2_ref[...], axis=-1, keepdims=True)
    return jnp.exp(s1) - jnp.exp(s2) + lam_init


def _rope_lanes(z, cs, s1, s2):
    outs = []
    for j in range(MIX_A // LANES):
        zj = z[:, j * LANES:(j + 1) * LANES]
        outs.append(zj * cs + pltpu.roll(zj, LANES - ROT_DIM // 2, 1) * s1
                    + pltpu.roll(zj, ROT_DIM // 2, 1) * s2)
    return jnp.concatenate(outs, axis=1)


def _rope_rows(zt, cos_t, sin_t):
    half = ROT_DIM // 2
    pieces = []
    for g in range(MIX_A // A_HEAD_DIM):
        r0 = g * A_HEAD_DIM
        x1, x2 = zt[r0:r0 + half], zt[r0 + half:r0 + ROT_DIM]
        pieces += [x1 * cos_t - x2 * sin_t, x2 * cos_t + x1 * sin_t, zt[r0 + ROT_DIM:r0 + A_HEAD_DIM]]
    return jnp.concatenate(pieces, axis=0)


Q_SCALE = A_HEAD_DIM ** -0.5 * LOG2E


def _inproj_prompt_kernel(x_ref, nw_ref, w_ref, wkt_ref, c_ref, s1_ref, s2_ref, ct_ref, st_ref,
                          q_ref, kt_ref, kbt_ref, v_ref, vb_ref, zh_ref):
    tm = x_ref.shape[0]
    h = _rms(x_ref[...], nw_ref[...], EPS).astype(BF16)
    q = _rope_lanes(_dot(h, w_ref[:, 0:MIX_A]), c_ref[...], s1_ref[...], s2_ref[...])
    q_ref[...] = (q * Q_SCALE).astype(BF16)
    kt = _rope_rows(_dot_nt(wkt_ref[...], h), ct_ref[...], st_ref[...])
    kt_ref[...] = kt
    kbt_ref[...] = kt.astype(BF16)
    v = _dot(h, w_ref[:, 2 * MIX_A:3 * MIX_A])
    for hh in range(A_HEADS):
        v_ref[pl.ds(hh, tm, stride=A_HEADS), :] = v[:, hh * LANES:(hh + 1) * LANES]
    vb_ref[...] = v.astype(BF16)
    zh_ref[...] = _dot(h, w_ref[:, 3 * MIX_A:N_IN])


def _inproj_prompt(x, norm_w, w_in_bf, wkt_bf, tabs, batch, seq, tm):
    n = x.shape[0]
    nb = seq // tm
    cs, s1, s2, ct, st = tabs
    row = lambda b, j: (b * nb + j, 0)
    const = lambda b, j: (0, 0)
    return pl.pallas_call(
        _inproj_prompt_kernel,
        grid=(batch, nb),
        in_specs=[
            pl.BlockSpec((tm, D_MODEL), row),
            pl.BlockSpec((1, D_MODEL), const),
            pl.BlockSpec((D_MODEL, N_IN), const, pipeline_mode=pl.Buffered(1)),
            pl.BlockSpec((MIX_A, D_MODEL), const, pipeline_mode=pl.Buffered(1)),
            pl.BlockSpec((tm, LANES), lambda b, j: (j, 0)),
            pl.BlockSpec((tm, LANES), lambda b, j: (j, 0)),
            pl.BlockSpec((tm, LANES), lambda b, j: (j, 0)),
            pl.BlockSpec((ROT_DIM // 2, tm), lambda b, j: (0, j)),
            pl.BlockSpec((ROT_DIM // 2, tm), lambda b, j: (0, j)),
        ],
        out_specs=[
            pl.BlockSpec((tm, MIX_A), row),
            pl.BlockSpec((None, MIX_A, tm), lambda b, j: (b, 0, j)),
            pl.BlockSpec((None, MIX_A, tm), lambda b, j: (b, 0, j)),
            pl.BlockSpec((tm * A_HEADS, LANES), row),
            pl.BlockSpec((tm, MIX_A), row),
            pl.BlockSpec((tm, 4 * MIX_H), row),
        ],
        out_shape=[
            jax.ShapeDtypeStruct((n, MIX_A), BF16),
            jax.ShapeDtypeStruct((batch, MIX_A, seq), F32),
            jax.ShapeDtypeStruct((batch, MIX_A, seq), BF16),
            jax.ShapeDtypeStruct((n * A_HEADS, LANES), F32),
            jax.ShapeDtypeStruct((n, MIX_A), BF16),
            jax.ShapeDtypeStruct((n, 4 * MIX_H), F32),
        ],
        compiler_params=pltpu.CompilerParams(
            dimension_semantics=("arbitrary", "arbitrary"), vmem_limit_bytes=VMEM_LIMIT),
        name="inproj_prompt",
    )(x, norm_w, w_in_bf, wkt_bf, cs, s1, s2, ct, st)


def _inproj_sample_kernel(x_ref, nw_ref, w_ref, c_ref, s1_ref, s2_ref, q_ref, k_ref, v_ref, zh_ref):
    h = _rms(x_ref[...], nw_ref[...], EPS).astype(BF16)
    cs, s1, s2 = c_ref[...], s1_ref[...], s2_ref[...]
    q_ref[...] = (_rope_lanes(_dot(h, w_ref[:, 0:MIX_A]), cs, s1, s2) * Q_SCALE).astype(BF16)
    k_ref[...] = _rope_lanes(_dot(h, w_ref[:, MIX_A:2 * MIX_A]), cs, s1, s2)
    v_ref[...] = _dot(h, w_ref[:, 2 * MIX_A:3 * MIX_A])
    zh_ref[...] = _dot(h, w_ref[:, 3 * MIX_A:N_IN])


def _inproj_sample(x, norm_w, w_in_bf, tabs, tm):
    n = x.shape[0]
    cs, s1, s2 = tabs
    row = lambda i: (i, 0)
    const = lambda i: (0, 0)
    return pl.pallas_call(
        _inproj_sample_kernel,
        grid=(n // tm,),
        in_specs=[
            pl.BlockSpec((tm, D_MODEL), row),
            pl.BlockSpec((1, D_MODEL), const),
            pl.BlockSpec((D_MODEL, N_IN), const),
            pl.BlockSpec((tm, LANES), row),
            pl.BlockSpec((tm, LANES), row),
            pl.BlockSpec((tm, LANES), row),
        ],
        out_specs=[
            pl.BlockSpec((tm, MIX_A), row),
            pl.BlockSpec((tm, MIX_A), row),
            pl.BlockSpec((tm, MIX_A), row),
            pl.BlockSpec((tm, 4 * MIX_H), row),
        ],
        out_shape=[
            jax.ShapeDtypeStruct((n, MIX_A), BF16),
            jax.ShapeDtypeStruct((n, MIX_A), F32),
            jax.ShapeDtypeStruct((n, MIX_A), F32),
            jax.ShapeDtypeStruct((n, 4 * MIX_H), F32),
        ],
        compiler_params=pltpu.CompilerParams(
            dimension_semantics=("arbitrary",), vmem_limit_bytes=VMEM_LIMIT),
        name="inproj_sample",
    )(x, norm_w, w_in_bf, cs, s1, s2)


def _rope_tables(pos):
    half = ROT_DIM // 2
    inv_freq = 1.0 / (ROPE_THETA ** (jnp.arange(half, dtype=F32) * 2.0 / ROT_DIM))
    ang = pos.astype(F32)[:, None] * inv_freq[None, :]
    cos, sin = jnp.cos(ang), jnp.sin(ang)
    n = pos.shape[0]
    ones = jnp.ones((n, A_HEAD_DIM - ROT_DIM), F32)
    zeros = jnp.zeros((n, A_HEAD_DIM - ROT_DIM), F32)
    zh = jnp.zeros((n, half), F32)
    cs = jnp.concatenate([cos, cos, ones], axis=1)
    s1 = jnp.concatenate([-sin, zh, zeros], axis=1)
    s2 = jnp.concatenate([zh, sin, zeros], axis=1)
    tile2 = lambda t: jnp.concatenate([t, t], axis=1)
    return tile2(cs), tile2(s1), tile2(s2), cos.T, sin.T


def _hgrn_masks():
    t = lax.broadcasted_iota(jnp.int32, (HGRN_CHUNK, HGRN_CHUNK), 0)
    s = lax.broadcasted_iota(jnp.int32, (HGRN_CHUNK, HGRN_CHUNK), 1)
    ts, ss = t // HGRN_SUB, s // HGRN_SUB
    m0 = jnp.where(ts == ss, jnp.where(s <= t, 1, 0), 0)
    m1 = jnp.where(ts == ss + 1, jnp.where(ts // 2 == ss // 2, 1, 0), 0)
    m2 = jnp.where(ts >= 2, jnp.where(ss < 2, 1, 0), 0)
    return m0, m1, m2


def _hgrn_prompt_kernel(layer, qh_ref, fh_ref, ih_ref, gh_ref, lbl_ref, nw_ref, o_ref, s_ref, st_ref):
    batch, tm = o_ref.shape[0], o_ref.shape[1]
    j = pl.program_id(0)

    @pl.when(j == 0)
    def _():
        st_ref[...] = jnp.zeros_like(st_ref)

    lb = _lower_bound(lbl_ref, layer)
    nw = nw_ref[...]
    m0, m1, m2 = _hgrn_masks()
    rows = lax.broadcasted_iota(jnp.int32, (HGRN_CHUNK, 1), 0)
    tr = lax.broadcasted_iota(jnp.int32, (HGRN_CHUNK, HGRN_CHUNK), 0)
    tc = lax.broadcasted_iota(jnp.int32, (HGRN_CHUNK, HGRN_CHUNK), 1)
    tri = jnp.where(tc <= tr, 1.0, 0.0).astype(BF16)

    def chunk(c, carry):
        for b in range(batch):
            one_chunk(c, b)
        return carry

    def one_chunk(c, b):
        r0 = pl.multiple_of(c * HGRN_CHUNK, HGRN_CHUNK)
        sl = pl.ds(r0, HGRN_CHUNK)
        qh, fh, v, gh = qh_ref[b, sl, :], fh_ref[b, sl, :], ih_ref[b, sl, :], gh_ref[b, sl, :]
        f = lb + (1.0 - lb) * _sigmoid(fh)
        g = jnp.log(f)
        q = qh * _sigmoid(qh)
        k = 1.0 - f
        ghi, gmid, glo = _split3(g)
        bc = _dot(tri, glo) + _dot(tri, gmid) + _dot(tri, ghi)
        r1 = bc[HGRN_SUB - 1:HGRN_SUB, :]
        r2 = bc[2 * HGRN_SUB - 1:2 * HGRN_SUB, :]
        r3 = bc[3 * HGRN_SUB - 1:3 * HGRN_SUB, :]
        blast = bc[HGRN_CHUNK - 1:HGRN_CHUNK, :]
        ref0 = jnp.where(rows < HGRN_SUB, 0.0,
                         jnp.where(rows < 2 * HGRN_SUB, r1, jnp.where(rows < 3 * HGRN_SUB, r2, r3)))
        ref1 = jnp.where(rows < 2 * HGRN_SUB, r1, r3)
        qs, ks = [], []
        for ref in (ref0, ref1, r2):
            d = bc - ref
            qs.append((q * jnp.exp(d)).astype(BF16))
            ks.append((k * jnp.exp(-d)).astype(BF16))
        q_in = (q * jnp.exp(bc)).astype(BF16)
        k_out = (k * jnp.exp(blast - bc)).astype(BF16)
        dec = jnp.exp(blast)
        vb = v.astype(BF16)
        outs = []
        for hh in range(H_HEADS):
            hs = slice(hh * H_EXPAND, (hh + 1) * H_EXPAND)
            a0 = _dot_nt(qs[0][:, hs], ks[0][:, hs])
            a1 = _dot_nt(qs[1][:, hs], ks[1][:, hs])
            a2 = _dot_nt(qs[2][:, hs], ks[2][:, hs])
            a = jnp.where(m0 == 1, a0, jnp.where(m1 == 1, a1, jnp.where(m2 == 1, a2, 0.0)))
            st = st_ref[b, hh]
            o = _dot_nt(q_in[:, hs], st.astype(BF16)) + _dot(a.astype(BF16), vb[:, hs])
            st_ref[b, hh] = st * dec[:, hs] + _dot_tn(vb[:, hs], k_out[:, hs])
            outs.append(o)
        oh = jnp.concatenate(outs, axis=1)
        o_ref[b, sl, :] = (_rms(oh, nw, EPS) * (gh * _sigmoid(gh))).astype(BF16)

    lax.fori_loop(0, tm // HGRN_CHUNK, chunk, 0)

    @pl.when(j == pl.num_programs(0) - 1)
    def _():
        for b in range(batch):
            for hh in range(H_HEADS):
                s_ref[b, hh] = st_ref[b, hh].T


def _hgrn_prompt(zh, lb_logits, norm_w, layer, batch, seq, tm):
    nb = seq // tm
    zh3 = zh.reshape(batch, seq, 4 * MIX_H)
    zspec = lambda cidx: pl.BlockSpec((batch, tm, MIX_H), lambda j: (0, j, cidx))
    const = lambda j: (0, 0)
    state_shape = (batch, H_HEADS, H_EXPAND, H_EXPAND)
    o, s_new = pl.pallas_call(
        functools.partial(_hgrn_prompt_kernel, layer),
        grid=(nb,),
        in_specs=[zspec(c) for c in range(4)] + [
            pl.BlockSpec(lb_logits.shape, const),
            pl.BlockSpec((1, MIX_H), const),
        ],
        out_specs=[
            pl.BlockSpec((batch, tm, MIX_H), lambda j: (0, j, 0)),
            pl.BlockSpec(state_shape, lambda j: (0, 0, 0, 0)),
        ],
        out_shape=[
            jax.ShapeDtypeStruct((batch, seq, MIX_H), BF16),
            jax.ShapeDtypeStruct(state_shape, F32),
        ],
        scratch_shapes=[pltpu.VMEM(state_shape, F32)],
        compiler_params=pltpu.CompilerParams(
            dimension_semantics=("arbitrary",), vmem_limit_bytes=VMEM_LIMIT),
        name="hgrn_prompt",
    )(zh3, zh3, zh3, zh3, lb_logits, norm_w)
    return o.reshape(batch * seq, MIX_H), s_new


def _hgrn_sample_kernel(layer, t_len, zh_ref, s0_ref, lbl_ref, nw_ref, o_ref, s_ref):
    rows_n = zh_ref.shape[0]
    bb = rows_n // t_len
    lb = _lower_bound(lbl_ref, layer)
    zh = zh_ref[...]
    qh, fh = zh[:, 0:MIX_H], zh[:, MIX_H:2 * MIX_H]
    v, gh = zh[:, 2 * MIX_H:3 * MIX_H], zh[:, 3 * MIX_H:4 * MIX_H]
    f = lb + (1.0 - lb) * _sigmoid(fh)
    g = jnp.log(f)
    q = qh * _sigmoid(qh)
    k = 1.0 - f
    tr = lax.broadcasted_iota(jnp.int32, (rows_n, rows_n), 0)
    tc = lax.broadcasted_iota(jnp.int32, (rows_n, rows_n), 1)
    same = tr // t_len == tc // t_len
    causal = jnp.where(same, jnp.where(tc <= tr, 1, 0), 0)
    tri = causal.astype(F32).astype(BF16)
    last = jnp.where(same, 1.0, 0.0).astype(BF16)
    ghi, gmid, glo = _split3(g)
    bc = _dot(tri, glo) + _dot(tri, gmid) + _dot(tri, ghi)
    blast = _dot(last, glo) + _dot(last, gmid) + _dot(last, ghi)
    q_in = (q * jnp.exp(bc)).astype(BF16)
    k_in = (k * jnp.exp(-bc)).astype(BF16)
    k_out = k * jnp.exp(blast - bc)
    vb = v.astype(BF16)
    rowid = lax.broadcasted_iota(jnp.int32, (rows_n, 1), 0)
    ones = jnp.ones((rows_n, LANES), BF16)
    outs = []
    for hh in range(H_HEADS):
        hs = slice(hh * H_EXPAND, (hh + 1) * H_EXPAND)
        a = jnp.where(causal == 1, _dot_nt(q_in[:, hs], k_in[:, hs]), 0.0)
        o = _dot(a.astype(BF16), vb[:, hs])
        for b in range(bb):
            mine = rowid // t_len == b
            s0 = s0_ref[b, hh]
            o = o + jnp.where(mine, _dot(q_in[:, hs], s0.astype(BF16)), 0.0)
            kb = jnp.where(mine, k_out[:, hs], 0.0).astype(BF16)
            dcol = (_dot_tn(jnp.where(mine, glo[:, hs], 0), ones)
                    + _dot_tn(jnp.where(mine, gmid[:, hs], 0), ones)
                    + _dot_tn(jnp.where(mine, ghi[:, hs], 0), ones))
            s_ref[b, hh] = s0 * jnp.exp(dcol) + _dot_tn(kb, vb[:, hs])
        outs.append(o)
    oh = jnp.concatenate(outs, axis=1)
    o_ref[...] = (_rms(oh, nw_ref[...], EPS) * (gh * _sigmoid(gh))).astype(BF16)


def _hgrn_sample(zh, s0, lb_logits, norm_w, layer, t_len, bb):
    n = zh.shape[0]
    nseq = n // t_len
    const = lambda i: (0, 0)
    return pl.pallas_call(
        functools.partial(_hgrn_sample_kernel, layer, t_len),
        grid=(nseq // bb,),
        in_specs=[
            pl.BlockSpec((bb * t_len, 4 * MIX_H), lambda i: (i, 0)),
            pl.BlockSpec((bb, H_HEADS, H_EXPAND, H_EXPAND), lambda i: (i, 0, 0, 0)),
            pl.BlockSpec(lb_logits.shape, const),
            pl.BlockSpec((1, MIX_H), const),
        ],
        out_specs=[
            pl.BlockSpec((bb * t_len, MIX_H), lambda i: (i, 0)),
            pl.BlockSpec((bb, H_HEADS, H_EXPAND, H_EXPAND), lambda i: (i, 0, 0, 0)),
        ],
        out_shape=[
            jax.ShapeDtypeStruct((n, MIX_H), BF16),
            jax.ShapeDtypeStruct(s0.shape, F32),
        ],
        compiler_params=pltpu.CompilerParams(
            dimension_semantics=("arbitrary",), vmem_limit_bytes=VMEM_LIMIT),
        name="hgrn_sample",
    )(zh, s0, lb_logits, norm_w)


def _attn_prompt_kernel(lam_init, tq, q_ref, kt_ref, v_ref, lq1, lk1, lq2, lk2, sw_ref,
                        o_ref, m_ref, acc_ref):
    tk = tq
    qi = pl.program_id(2)
    q = q_ref[...]
    lane = lax.broadcasted_iota(jnp.int32, (1, LANES), 1)
    qc = [jnp.where((lane // A_HEAD_DIM) == c, q, jnp.zeros_like(q)) for c in range(2)]
    m_ref[...] = jnp.full_like(m_ref, -jnp.inf)
    acc_ref[...] = jnp.zeros_like(acc_ref)

    def step(ki, masked):
        r0 = pl.multiple_of(ki * tk, tk)
        kt = kt_ref[:, pl.ds(r0, tk)]
        v = v_ref[pl.ds(r0, tk), :]
        for c in range(2):
            s = _dot(qc[c], kt)
            if masked:
                rr = lax.broadcasted_iota(jnp.int32, (tq, tk), 0)
                cc = lax.broadcasted_iota(jnp.int32, (tq, tk), 1)
                s = jnp.where(cc <= rr, s, -jnp.inf)
            m_prev = m_ref[c]
            m_next = jnp.maximum(m_prev, jnp.max(s, axis=1, keepdims=True))
            alpha = jnp.exp2(m_prev - m_next)
            p = jnp.exp2(s - pltpu.repeat(m_next, tk // LANES, axis=1))
            a = acc_ref[c]
            pv = _dot(p.astype(BF16), v)
            l = a[:, LANES:] * alpha + jnp.sum(p, axis=1, keepdims=True)
            acc_ref[c] = jnp.concatenate([a[:, :LANES] * alpha + pv, l], axis=1)
            m_ref[c] = m_next

    def body(ki, carry):
        step(ki, False)
        return carry

    lax.fori_loop(0, qi, body, 0)
    step(qi, True)

    lam = _lambda(lq1, lk1, lq2, lk2, lam_init)
    a0, a1 = acc_ref[0], acc_ref[1]
    o = a0[:, :LANES] / a0[:, LANES:] - lam * (a1[:, :LANES] / a1[:, LANES:])
    o_ref[...] = (_rms(o, sw_ref[...], SUBLN_EPS) * (1.0 - lam_init)).astype(BF16)


def _attn_prompt(q, kbt, vb, lams, subln_w, lam_init, batch, seq, tq):
    n = q.shape[0]
    nq = seq // tq
    vb3 = vb.reshape(batch, seq, MIX_A)
    const = lambda b, h, i: (0, 0)
    return pl.pallas_call(
        functools.partial(_attn_prompt_kernel, lam_init, tq),
        grid=(batch, A_HEADS, nq),
        in_specs=[
            pl.BlockSpec((tq, LANES), lambda b, h, i: (b * nq + i, h)),
            pl.BlockSpec((None, LANES, seq), lambda b, h, i: (b, h, 0)),
            pl.BlockSpec((None, seq, LANES), lambda b, h, i: (b, 0, h)),
        ] + [pl.BlockSpec((1, A_HEAD_DIM), const)] * 4 + [
            pl.BlockSpec((1, 2 * A_HEAD_DIM), const),
        ],
        out_specs=pl.BlockSpec((tq, LANES), lambda b, h, i: (b * nq + i, h)),
        out_shape=jax.ShapeDtypeStruct((n, MIX_A), BF16),
        scratch_shapes=[
            pltpu.VMEM((2, tq, LANES), F32),
            pltpu.VMEM((2, tq, 2 * LANES), F32),
        ],
        compiler_params=pltpu.CompilerParams(
            dimension_semantics=("arbitrary", "arbitrary", "arbitrary"),
            vmem_limit_bytes=VMEM_LIMIT),
        name="attn_prompt",
    )(q, kbt, vb3, *lams, subln_w)


def _attn_sample_kernel(lam_init, n_pages, t_len, pt_ref, q_ref, kn_ref, vn_ref, *rest):
    kp_refs = rest[:n_pages]
    vp_refs = rest[n_pages:2 * n_pages]
    lq1, lk1, lq2, lk2, sw_ref, o_ref = rest[2 * n_pages:]
    nrow = 2 * A_HEADS * t_len
    r = lax.broadcasted_iota(jnp.int32, (nrow, 1), 0)
    r_tok = r % t_len
    r_grp = (r // t_len) % A_HEADS * 2 + r // (A_HEADS * t_len)
    lane = lax.broadcasted_iota(jnp.int32, (1, MIX_A), 1)
    q4 = q_ref[...].astype(F32)
    qt = jnp.zeros((nrow, MIX_A), F32)
    for t in range(t_len):
        qt = jnp.where(r_tok == t, q4[t:t + 1, :], qt)
    qt = jnp.where(lane // A_HEAD_DIM == r_grp, qt, 0.0)
    qtb = qt.astype(BF16)

    s_past = [_dot(qtb, kp[...].astype(BF16)) for kp in kp_refs]
    kn = kn_ref[...]
    s_new = []
    for t in range(t_len):
        sn = jnp.sum(qt * kn[t:t + 1, :], axis=-1, keepdims=True)
        s_new.append(jnp.where(r_tok >= t, sn, -jnp.inf))
    m = s_new[0]
    for sp in s_past:
        m = jnp.maximum(m, jnp.max(sp, axis=-1, keepdims=True))
    for sn in s_new[1:]:
        m = jnp.maximum(m, sn)
    p_past = [jnp.exp2(sp - m) for sp in s_past]
    p_new = [jnp.exp2(sn - m) for sn in s_new]
    l = p_new[0]
    for pn in p_new[1:]:
        l = l + pn
    for pp in p_past:
        l = l + jnp.sum(pp, axis=-1, keepdims=True)
    inv = 1.0 / l
    lam = _lambda(lq1, lk1, lq2, lk2, lam_init)
    half = nrow // 2
    w = jnp.where(r < half, inv, -lam * inv)

    def combine(p):
        pw = p * w
        return pw[:half] + pw[half:]

    pd_past = jnp.concatenate([combine(pp) for pp in p_past], axis=1).astype(BF16)
    pd_new = [combine(pn) for pn in p_new]
    vn = vn_ref[...]
    page = vp_refs[0].shape[0] // A_HEADS
    outs = []
    for hh in range(A_HEADS):
        vh = jnp.concatenate(
            [vp[pl.ds(hh, page, stride=A_HEADS), :].astype(BF16) for vp in vp_refs], axis=0)
        hs = slice(hh * LANES, (hh + 1) * LANES)
        res = _dot(pd_past, vh)
        for t in range(t_len):
            res = res + pd_new[t] * vn[t:t + 1, hs]
        o = res[hh * t_len:(hh + 1) * t_len]
        outs.append(_rms(o, sw_ref[...], SUBLN_EPS) * (1.0 - lam_init))
    o_ref[...] = jnp.concatenate(outs, axis=1).astype(BF16)


def _attn_sample(q, k_new, v_new, cache_k, cache_v, page_table, lams, subln_w, lam_init, t_len):
    nseq, n_pages = page_table.shape
    n_pool, page = cache_k.shape[0], cache_k.shape[1]
    ck = jnp.transpose(cache_k, (0, 2, 3, 4, 1)).reshape(n_pool, MIX_A, page)
    cv = cache_v.reshape(n_pool, page * A_HEADS, LANES)
    q3 = q.reshape(nseq, t_len, MIX_A)
    k3 = k_new.reshape(nseq, t_len, MIX_A)
    v3 = v_new.reshape(nseq, t_len, MIX_A)
    tok = pl.BlockSpec((None, t_len, MIX_A), lambda b, pt: (b, 0, 0))
    const = lambda b, pt: (0, 0)
    kpage = lambda j: pl.BlockSpec((None, MIX_A, page), lambda b, pt: (pt[b, j], 0, 0))
    vpage = lambda j: pl.BlockSpec((None, page * A_HEADS, LANES), lambda b, pt: (pt[b, j], 0, 0))
    grid_spec = pltpu.PrefetchScalarGridSpec(
        num_scalar_prefetch=1,
        grid=(nseq,),
        in_specs=[tok, tok, tok]
        + [kpage(j) for j in range(n_pages)]
        + [vpage(j) for j in range(n_pages)]
        + [pl.BlockSpec((1, A_HEAD_DIM), const)] * 4
        + [pl.BlockSpec((1, 2 * A_HEAD_DIM), const)],
        out_specs=tok,
    )
    out = pl.pallas_call(
        functools.partial(_attn_sample_kernel, lam_init, n_pages, t_len),
        grid_spec=grid_spec,
        out_shape=jax.ShapeDtypeStruct((nseq, t_len, MIX_A), BF16),
        compiler_params=pltpu.CompilerParams(
            dimension_semantics=("arbitrary",), vmem_limit_bytes=VMEM_LIMIT),
        name="attn_sample",
    )(page_table, q3, k3, v3, *([ck] * n_pages), *([cv] * n_pages), *lams, subln_w)
    return out.reshape(nseq * t_len, MIX_A)


def _out_ffn_kernel(x_ref, ma_ref, mh_ref, wo_ref, fnw_ref, wg_ref, wu_ref, wd_ref, finw_ref, y_ref):
    x2 = x_ref[...] + _dot(ma_ref[...], wo_ref[0:MIX_A, :]) + _dot(mh_ref[...], wo_ref[MIX_A:, :])
    h2 = _rms(x2, fnw_ref[...], EPS).astype(BF16)
    acc = jnp.zeros_like(x2)
    for c in range(D_FF // FF_CHUNK):
        cs = slice(c * FF_CHUNK, (c + 1) * FF_CHUNK)
        g = _dot(h2, wg_ref[:, cs])
        u = _dot(h2, wu_ref[:, cs])
        acc = acc + _dot((g * _sigmoid(g) * u).astype(BF16), wd_ref[cs, :])
    y_ref[...] = _rms(x2 + acc, finw_ref[...], EPS)


def _out_ffn(x, mix_a, mix_h, wo, fnw, wg, wu, wd, finw, tm):
    n = x.shape[0]
    row = lambda i: (i, 0)
    const = lambda i: (0, 0)
    wspec = lambda shape: pl.BlockSpec(shape, const, pipeline_mode=pl.Buffered(1))
    return pl.pallas_call(
        _out_ffn_kernel,
        grid=(n // tm,),
        in_specs=[
            pl.BlockSpec((tm, D_MODEL), row),
            pl.BlockSpec((tm, MIX_A), row),
            pl.BlockSpec((tm, MIX_H), row),
            wspec((MIX_A + MIX_H, D_MODEL)),
            pl.BlockSpec((1, D_MODEL), const),
            wspec((D_MODEL, D_FF)),
            wspec((D_MODEL, D_FF)),
            wspec((D_FF, D_MODEL)),
            pl.BlockSpec((1, D_MODEL), const),
        ],
        out_specs=pl.BlockSpec((tm, D_MODEL), row),
        out_shape=jax.ShapeDtypeStruct((n, D_MODEL), F32),
        compiler_params=pltpu.CompilerParams(
            dimension_semantics=("arbitrary",), vmem_limit_bytes=VMEM_LIMIT),
        name="out_ffn",
    )(x, mix_a, mix_h, wo, fnw, wg, wu, wd, finw)


def _pick(n, prefs):
    for p in prefs:
        if n % p == 0:
            return p
    return n


def kernel(x_prompt, x_sample, cache_k, cache_v, state_hgrn, page_table, attn_norm, w_in, w_out,
           lambda_q1, lambda_k1, lambda_q2, lambda_k2, subln_w, hgrn_lb_logits, hgrn_norm,
           ffn_norm, w_gate, w_up, w_down, final_norm):
    batch, seq, _ = x_prompt.shape
    nseq, t_len, _ = x_sample.shape
    n_pages = page_table.shape[1]
    page = cache_k.shape[2]
    depth = w_in.shape[0]
    assert depth == 1, "single-layer step"
    past_len = n_pages * page
    layer = 0
    lam_init = 0.8 - 0.6 * math.exp(-0.3 * layer)

    row2 = lambda a: a.reshape(1, -1)
    w_in_bf = w_in[layer].astype(BF16)
    wo = w_out[layer].astype(BF16)
    wg = w_gate[layer].astype(BF16)
    wu = w_up[layer].astype(BF16)
    wd = w_down[layer].astype(BF16)
    lams = [row2(a[layer]) for a in (lambda_q1, lambda_k1, lambda_q2, lambda_k2)]
    sw = row2(subln_w[layer])
    an, hn, fn, finw = row2(attn_norm[layer]), row2(hgrn_norm[layer]), row2(ffn_norm[layer]), row2(final_norm)

    n_p = batch * seq
    xp = x_prompt.reshape(n_p, D_MODEL)
    tm_p = _pick(seq, (512, 256, 128, 64))
    wkt_bf = w_in[layer][:, MIX_A:2 * MIX_A].T.astype(BF16)
    q, kt_new, kbt, v_new, vb, zh = _inproj_prompt(
        xp, an, w_in_bf, wkt_bf, _rope_tables(jnp.arange(seq)), batch, seq, tm_p)
    mix_h, s_prompt = _hgrn_prompt(zh, hgrn_lb_logits, hn, layer, batch, seq, _pick(seq, (512, 256, 128, 64)))
    mix_a = _attn_prompt(q, kbt, vb, lams, sw, lam_init, batch, seq, _pick(seq, (512, 256, 128)))
    y_prompt = _out_ffn(xp, mix_a, mix_h, wo, fn, wg, wu, wd, finw,
                        _pick(seq, (512, 256, 128, 64))).reshape(batch, seq, D_MODEL)
    k_new = jnp.transpose(kt_new.reshape(batch, A_HEADS, 2, A_HEAD_DIM, seq), (0, 4, 1, 2, 3))

    n_s = nseq * t_len
    xs = x_sample.reshape(n_s, D_MODEL)
    tm_s = _pick(n_s, (256, 128, 64, 32, 16, 8))
    qs, ks_new, vs_new, zhs = _inproj_sample(
        xs, an, w_in_bf, _rope_tables(past_len + jnp.arange(n_s) % t_len)[:3], tm_s)
    bb = _pick(nseq, (8, 4, 2, 1))
    mix_h_s, s_sample = _hgrn_sample(zhs, state_hgrn[layer].astype(F32), hgrn_lb_logits, hn, layer, t_len, bb)
    mix_a_s = _attn_sample(qs, ks_new, vs_new, cache_k[layer], cache_v[layer], page_table, lams, sw,
                           lam_init, t_len)
    y_sample = _out_ffn(xs, mix_a_s, mix_h_s, wo, fn, wg, wu, wd, finw, tm_s).reshape(nseq, t_len, D_MODEL)

    return (y_prompt, y_sample,
            k_new[None],
            v_new.reshape(1, batch, seq, A_HEADS, 2 * A_HEAD_DIM),
            s_prompt[None],
            ks_new.reshape(1, nseq, t_len, A_HEADS, 2, A_HEAD_DIM),
            vs_new.reshape(1, nseq, t_len, A_HEADS, 2 * A_HEAD_DIM),
            s_sample[None])
```

```python
import functools
import math

import jax
import jax.numpy as jnp
from jax import lax
from jax.experimental import pallas as pl
from jax.experimental.pallas import tpu as pltpu

D_MODEL = 1024
MIX_A = 512
MIX_H = 512
A_HEAD_DIM = 64
A_HEADS = 4
ROT_DIM = 16
ROPE_THETA = 500000.0
H_EXPAND = 128
H_HEADS = 4
D_FF = 2816
N_IN = 3 * MIX_A + 4 * MIX_H
EPS = 1e-6
SUBLN_EPS = 1e-5
LOG2E = 1.4426950408889634

LANES = 128
HGRN_CHUNK = 64
HGRN_SUB = 16
HGRN_UNROLL = 2
FF_CHUNK = 256
VMEM_LIMIT = 56 * 1024 * 1024

F32 = jnp.float32
BF16 = jnp.bfloat16


def _dot(a, b):
    return jnp.dot(a, b, preferred_element_type=F32)


def _dot_nt(a, b):
    return lax.dot_general(a, b, (((1,), (1,)), ((), ())), preferred_element_type=F32)


def _dot_tn(a, b):
    return lax.dot_general(a, b, (((0,), (0,)), ((), ())), preferred_element_type=F32)


def _split3(a):
    hi = a.astype(BF16)
    r = a - hi.astype(F32)
    mid = r.astype(BF16)
    lo = (r - mid.astype(F32)).astype(BF16)
    return hi, mid, lo


def _sigmoid(x):
    return 1.0 / (1.0 + jnp.exp(-x))


def _rms(x, w, eps):
    ms = jnp.mean(x * x, axis=-1, keepdims=True)
    return x * lax.rsqrt(ms + eps) * w


def _lower_bound(lb_logits_ref, layer):
    lg = lb_logits_ref[...]
    e = jnp.exp(lg - jnp.max(lg, axis=0, keepdims=True))
    return jnp.sum(e[: layer + 1], axis=0, keepdims=True) / jnp.sum(e, axis=0, keepdims=True)


def _lambda(lq1_ref, lk1_ref, lq2_ref, lk2_ref, lam_init):
    s1 = jnp.sum(lq1_ref[...] * lk1_ref[...], axis=-1, keepdims=True)
    s2 = jnp.sum(lq2_ref[...] * lk2_ref[...], axis=-1, keepdims=True)
    return jnp.exp(s1) - jnp.exp(s2) + lam_init


def _rope_lanes(z, cs, s1, s2):
    outs = []
    for j in range(MIX_A // LANES):
        zj = z[:, j * LANES:(j + 1) * LANES]
        outs.append(zj * cs + pltpu.roll(zj, LANES - ROT_DIM // 2, 1) * s1
                    + pltpu.roll(zj, ROT_DIM // 2, 1) * s2)
    return jnp.concatenate(outs, axis=1)


def _rope_rows(zt, cos_t, sin_t):
    half = ROT_DIM // 2
    pieces = []
    for g in range(MIX_A // A_HEAD_DIM):
        r0 = g * A_HEAD_DIM
        x1, x2 = zt[r0:r0 + half], zt[r0 + half:r0 + ROT_DIM]
        pieces += [x1 * cos_t - x2 * sin_t, x2 * cos_t + x1 * sin_t, zt[r0 + ROT_DIM:r0 + A_HEAD_DIM]]
    return jnp.concatenate(pieces, axis=0)


Q_SCALE = A_HEAD_DIM ** -0.5 * LOG2E


def _inproj_prompt_kernel(x_ref, nw_ref, w_ref, wkt_ref, c_ref, s1_ref, s2_ref, ct_ref, st_ref,
                          q_ref, kt_ref, kbt_ref, v_ref, vb_ref, zh_ref):
    tm = x_ref.shape[0]
    h = _rms(x_ref[...], nw_ref[...], EPS).astype(BF16)
    q = _rope_lanes(_dot(h, w_ref[:, 0:MIX_A]), c_ref[...], s1_ref[...], s2_ref[...])
    q_ref[...] = (q * Q_SCALE).astype(BF16)
    kt = _rope_rows(_dot_nt(wkt_ref[...], h), ct_ref[...], st_ref[...])
    kt_ref[...] = kt
    kbt_ref[...] = kt.astype(BF16)
    v = _dot(h, w_ref[:, 2 * MIX_A:3 * MIX_A])
    for hh in range(A_HEADS):
        v_ref[pl.ds(hh, tm, stride=A_HEADS), :] = v[:, hh * LANES:(hh + 1) * LANES]
    vb_ref[...] = v.astype(BF16)
    zh_ref[...] = _dot(h, w_ref[:, 3 * MIX_A:N_IN])


def _inproj_prompt(x, norm_w, w_in_bf, wkt_bf, tabs, batch, seq, tm):
    n = x.shape[0]
    nb = seq // tm
    cs, s1, s2, ct, st = tabs
    row = lambda b, j: (b * nb + j, 0)
    const = lambda b, j: (0, 0)
    return pl.pallas_call(
        _inproj_prompt_kernel,
        grid=(batch, nb),
        in_specs=[
            pl.BlockSpec((tm, D_MODEL), row),
            pl.BlockSpec((1, D_MODEL), const),
            pl.BlockSpec((D_MODEL, N_IN), const, pipeline_mode=pl.Buffered(1)),
            pl.BlockSpec((MIX_A, D_MODEL), const, pipeline_mode=pl.Buffered(1)),
            pl.BlockSpec((tm, LANES), lambda b, j: (j, 0)),
            pl.BlockSpec((tm, LANES), lambda b, j: (j, 0)),
            pl.BlockSpec((tm, LANES), lambda b, j: (j, 0)),
            pl.BlockSpec((ROT_DIM // 2, tm), lambda b, j: (0, j)),
            pl.BlockSpec((ROT_DIM // 2, tm), lambda b, j: (0, j)),
        ],
        out_specs=[
            pl.BlockSpec((tm, MIX_A), row),
            pl.BlockSpec((None, MIX_A, tm), lambda b, j: (b, 0, j)),
            pl.BlockSpec((None, MIX_A, tm), lambda b, j: (b, 0, j)),
            pl.BlockSpec((tm * A_HEADS, LANES), row),
            pl.BlockSpec((tm, MIX_A), row),
            pl.BlockSpec((tm, 4 * MIX_H), row),
        ],
        out_shape=[
            jax.ShapeDtypeStruct((n, MIX_A), BF16),
            jax.ShapeDtypeStruct((batch, MIX_A, seq), F32),
            jax.ShapeDtypeStruct((batch, MIX_A, seq), BF16),
            jax.ShapeDtypeStruct((n * A_HEADS, LANES), F32),
            jax.ShapeDtypeStruct((n, MIX_A), BF16),
            jax.ShapeDtypeStruct((n, 4 * MIX_H), F32),
        ],
        compiler_params=pltpu.CompilerParams(
            dimension_semantics=("arbitrary", "arbitrary"), vmem_limit_bytes=VMEM_LIMIT),
        name="inproj_prompt",
    )(x, norm_w, w_in_bf, wkt_bf, cs, s1, s2, ct, st)


def _inproj_sample_kernel(x_ref, nw_ref, w_ref, c_ref, s1_ref, s2_ref, q_ref, k_ref, v_ref, zh_ref):
    h = _rms(x_ref[...], nw_ref[...], EPS).astype(BF16)
    cs, s1, s2 = c_ref[...], s1_ref[...], s2_ref[...]
    q_ref[...] = (_rope_lanes(_dot(h, w_ref[:, 0:MIX_A]), cs, s1, s2) * Q_SCALE).astype(BF16)
    k_ref[...] = _rope_lanes(_dot(h, w_ref[:, MIX_A:2 * MIX_A]), cs, s1, s2)
    v_ref[...] = _dot(h, w_ref[:, 2 * MIX_A:3 * MIX_A])
    zh_ref[...] = _dot(h, w_ref[:, 3 * MIX_A:N_IN])


def _inproj_sample(x, norm_w, w_in_bf, tabs, tm):
    n = x.shape[0]
    cs, s1, s2 = tabs
    row = lambda i: (i, 0)
    const = lambda i: (0, 0)
    return pl.pallas_call(
        _inproj_sample_kernel,
        grid=(n // tm,),
        in_specs=[
            pl.BlockSpec((tm, D_MODEL), row),
            pl.BlockSpec((1, D_MODEL), const),
            pl.BlockSpec((D_MODEL, N_IN), const),
            pl.BlockSpec((tm, LANES), row),
            pl.BlockSpec((tm, LANES), row),
            pl.BlockSpec((tm, LANES), row),
        ],
        out_specs=[
            pl.BlockSpec((tm, MIX_A), row),
            pl.BlockSpec((tm, MIX_A), row),
            pl.BlockSpec((tm, MIX_A), row),
            pl.BlockSpec((tm, 4 * MIX_H), row),
        ],
        out_shape=[
            jax.ShapeDtypeStruct((n, MIX_A), BF16),
            jax.ShapeDtypeStruct((n, MIX_A), F32),
            jax.ShapeDtypeStruct((n, MIX_A), F32),
            jax.ShapeDtypeStruct((n, 4 * MIX_H), F32),
        ],
        compiler_params=pltpu.CompilerParams(
            dimension_semantics=("arbitrary",), vmem_limit_bytes=VMEM_LIMIT),
        name="inproj_sample",
    )(x, norm_w, w_in_bf, cs, s1, s2)


def _rope_tables(pos):
    half = ROT_DIM // 2
    inv_freq = 1.0 / (ROPE_THETA ** (jnp.arange(half, dtype=F32) * 2.0 / ROT_DIM))
    ang = pos.astype(F32)[:, None] * inv_freq[None, :]
    cos, sin = jnp.cos(ang), jnp.sin(ang)
    lane = jnp.arange(LANES)
    d = lane % A_HEAD_DIM
    hit = (lane[None, :] % half == jnp.arange(half)[:, None]).astype(F32)
    e_cos = hit * (d < ROT_DIM)
    e_s1 = -hit * (d < half)
    e_s2 = hit * ((d >= half) & (d < ROT_DIM))
    spread = lambda t, e: jnp.dot(t, e, precision=lax.Precision.HIGHEST)
    cs = spread(cos, e_cos) + (d >= ROT_DIM).astype(F32)[None, :]
    return cs, spread(sin, e_s1), spread(sin, e_s2), cos.T, sin.T


def _hgrn_masks():
    t = lax.broadcasted_iota(jnp.int32, (HGRN_CHUNK, 2 * HGRN_CHUNK), 0)
    lane = lax.broadcasted_iota(jnp.int32, (HGRN_CHUNK, 2 * HGRN_CHUNK), 1)
    left = lane < HGRN_CHUNK
    s = jnp.where(left, lane, lane - HGRN_CHUNK)
    ts, ss = t // HGRN_SUB, s // HGRN_SUB
    m0 = jnp.where(left, jnp.where(ts == ss, jnp.where(s <= t, 1, 0), 0), 0)
    m1 = jnp.where(left, 0, jnp.where(ts == ss + 1, jnp.where(ts // 2 == ss // 2, 1, 0), 0))
    m2 = jnp.where(left, jnp.where(ts >= 2, jnp.where(ss < 2, 1, 0), 0), 0)
    return m0, m1, m2


def _hgrn_prompt_kernel(layer, qh_ref, fh_ref, ih_ref, gh_ref, lbl_ref, nw_ref, o_ref, s_ref, st_ref):
    batch, tm = o_ref.shape[0], o_ref.shape[1]
    j = pl.program_id(0)

    @pl.when(j == 0)
    def _():
        st_ref[...] = jnp.zeros_like(st_ref)

    lb = _lower_bound(lbl_ref, layer)
    nw = nw_ref[...]
    m0, m1, m2 = _hgrn_masks()
    rows = lax.broadcasted_iota(jnp.int32, (HGRN_CHUNK, 1), 0)
    tr = lax.broadcasted_iota(jnp.int32, (HGRN_CHUNK, HGRN_CHUNK), 0)
    tc = lax.broadcasted_iota(jnp.int32, (HGRN_CHUNK, HGRN_CHUNK), 1)
    tri = jnp.where(tc <= tr, 1.0, 0.0).astype(BF16)

    def chunk(c, carry):
        for u in range(HGRN_UNROLL):
            for b in range(batch):
                one_chunk(c * HGRN_UNROLL + u, b)
        return carry

    def one_chunk(c, b):
        r0 = pl.multiple_of(c * HGRN_CHUNK, HGRN_CHUNK)
        sl = pl.ds(r0, HGRN_CHUNK)
        qh, fh, v, gh = qh_ref[b, sl, :], fh_ref[b, sl, :], ih_ref[b, sl, :], gh_ref[b, sl, :]
        f = lb + (1.0 - lb) * _sigmoid(fh)
        g = jnp.log(f)
        q = qh * _sigmoid(qh)
        k = 1.0 - f
        ghi, gmid, glo = _split3(g)
        bc3 = _dot(tri, jnp.concatenate([glo, gmid, ghi], axis=1))
        bc = bc3[:, 0:MIX_H] + bc3[:, MIX_H:2 * MIX_H] + bc3[:, 2 * MIX_H:]
        r1 = bc[HGRN_SUB - 1:HGRN_SUB, :]
        r2 = bc[2 * HGRN_SUB - 1:2 * HGRN_SUB, :]
        r3 = bc[3 * HGRN_SUB - 1:3 * HGRN_SUB, :]
        blast = bc[HGRN_CHUNK - 1:HGRN_CHUNK, :]
        ref0 = jnp.where(rows < HGRN_SUB, 0.0,
                         jnp.where(rows < 2 * HGRN_SUB, r1, jnp.where(rows < 3 * HGRN_SUB, r2, r3)))
        ref1 = jnp.where(rows < 2 * HGRN_SUB, r1, r3)
        qs, ks = [], []
        for ref in (ref0, ref1, r2):
            d = bc - ref
            qs.append((q * jnp.exp(d)).astype(BF16))
            ks.append((k * jnp.exp(-d)).astype(BF16))
        q_in = (q * jnp.exp(bc)).astype(BF16)
        k_out = (k * jnp.exp(blast - bc)).astype(BF16)
        dec = jnp.exp(blast)
        vb = v.astype(BF16)
        qcat = jnp.concatenate(qs, axis=0)
        kcat = jnp.concatenate(ks + [jnp.zeros_like(ks[0])], axis=0)
        vv = jnp.concatenate([vb, vb], axis=0)
        outs = []
        for hh in range(H_HEADS):
            hs = slice(hh * H_EXPAND, (hh + 1) * H_EXPAND)
            aa = _dot_nt(qcat[:, hs], kcat[:, hs])
            top = aa[0:HGRN_CHUNK, 0:2 * HGRN_CHUNK]
            mid = aa[HGRN_CHUNK:2 * HGRN_CHUNK, 0:2 * HGRN_CHUNK]
            bot = aa[2 * HGRN_CHUNK:, 2 * HGRN_CHUNK:]
            a = jnp.where(m0 == 1, top, jnp.where(m1 == 1, mid, jnp.where(m2 == 1, bot, 0.0)))
            st = st_ref[b, hh]
            o = _dot_nt(q_in[:, hs], st.astype(BF16)) + _dot(a.astype(BF16), vv[:, hs])
            st_ref[b, hh] = st * dec[:, hs] + _dot_tn(vb[:, hs], k_out[:, hs])
            outs.append(o)
        oh = jnp.concatenate(outs, axis=1)
        o_ref[b, sl, :] = (_rms(oh, nw, EPS) * (gh * _sigmoid(gh))).astype(BF16)

    lax.fori_loop(0, tm // (HGRN_CHUNK * HGRN_UNROLL), chunk, 0)

    @pl.when(j == pl.num_programs(0) - 1)
    def _():
        for b in range(batch):
            for hh in range(H_HEADS):
                s_ref[b, hh] = st_ref[b, hh].T


def _hgrn_prompt(zh, lb_logits, norm_w, layer, batch, seq, tm):
    nb = seq // tm
    zh3 = zh.reshape(batch, seq, 4 * MIX_H)
    zspec = lambda cidx: pl.BlockSpec((batch, tm, MIX_H), lambda j: (0, j, cidx))
    const = lambda j: (0, 0)
    state_shape = (batch, H_HEADS, H_EXPAND, H_EXPAND)
    o, s_new = pl.pallas_call(
        functools.partial(_hgrn_prompt_kernel, layer),
        grid=(nb,),
        in_specs=[zspec(c) for c in range(4)] + [
            pl.BlockSpec(lb_logits.shape, const),
            pl.BlockSpec((1, MIX_H), const),
        ],
        out_specs=[
            pl.BlockSpec((batch, tm, MIX_H), lambda j: (0, j, 0)),
            pl.BlockSpec(state_shape, lambda j: (0, 0, 0, 0)),
        ],
        out_shape=[
            jax.ShapeDtypeStruct((batch, seq, MIX_H), BF16),
            jax.ShapeDtypeStruct(state_shape, F32),
        ],
        scratch_shapes=[pltpu.VMEM(state_shape, F32)],
        compiler_params=pltpu.CompilerParams(
            dimension_semantics=("arbitrary",), vmem_limit_bytes=VMEM_LIMIT),
        name="hgrn_prompt",
    )(zh3, zh3, zh3, zh3, lb_logits, norm_w)
    return o.reshape(batch * seq, MIX_H), s_new


def _hgrn_sample_kernel(layer, t_len, zh_ref, s0_ref, lbl_ref, nw_ref, o_ref, s_ref):
    rows_n = zh_ref.shape[0]
    bb = rows_n // t_len
    lb = _lower_bound(lbl_ref, layer)
    zh = zh_ref[...]
    qh, fh = zh[:, 0:MIX_H], zh[:, MIX_H:2 * MIX_H]
    v, gh = zh[:, 2 * MIX_H:3 * MIX_H], zh[:, 3 * MIX_H:4 * MIX_H]
    f = lb + (1.0 - lb) * _sigmoid(fh)
    g = jnp.log(f)
    q = qh * _sigmoid(qh)
    k = 1.0 - f
    tr = lax.broadcasted_iota(jnp.int32, (rows_n, rows_n), 0)
    tc = lax.broadcasted_iota(jnp.int32, (rows_n, rows_n), 1)
    same = tr // t_len == tc // t_len
    causal = jnp.where(same, jnp.where(tc <= tr, 1, 0), 0)
    tri = causal.astype(F32).astype(BF16)
    last = jnp.where(same, 1.0, 0.0).astype(BF16)
    ghi, gmid, glo = _split3(g)
    bc = _dot(tri, glo) + _dot(tri, gmid) + _dot(tri, ghi)
    blast = _dot(last, glo) + _dot(last, gmid) + _dot(last, ghi)
    q_in = (q * jnp.exp(bc)).astype(BF16)
    k_in = (k * jnp.exp(-bc)).astype(BF16)
    k_out = k * jnp.exp(blast - bc)
    vb = v.astype(BF16)
    rowid = lax.broadcasted_iota(jnp.int32, (rows_n, 1), 0)
    ones = jnp.ones((rows_n, LANES), BF16)
    outs = []
    for hh in range(H_HEADS):
        hs = slice(hh * H_EXPAND, (hh + 1) * H_EXPAND)
        a = jnp.where(causal == 1, _dot_nt(q_in[:, hs], k_in[:, hs]), 0.0)
        o = _dot(a.astype(BF16), vb[:, hs])
        for b in range(bb):
            mine = rowid // t_len == b
            s0 = s0_ref[b, hh]
            o = o + jnp.where(mine, _dot(q_in[:, hs], s0.astype(BF16)), 0.0)
            kb = jnp.where(mine, k_out[:, hs], 0.0).astype(BF16)
            dcol = (_dot_tn(jnp.where(mine, glo[:, hs], 0), ones)
                    + _dot_tn(jnp.where(mine, gmid[:, hs], 0), ones)
                    + _dot_tn(jnp.where(mine, ghi[:, hs], 0), ones))
            s_ref[b, hh] = s0 * jnp.exp(dcol) + _dot_tn(kb, vb[:, hs])
        outs.append(o)
    oh = jnp.concatenate(outs, axis=1)
    o_ref[...] = (_rms(oh, nw_ref[...], EPS) * (gh * _sigmoid(gh))).astype(BF16)


def _hgrn_sample(zh, s0, lb_logits, norm_w, layer, t_len, bb):
    n = zh.shape[0]
    nseq = n // t_len
    const = lambda i: (0, 0)
    return pl.pallas_call(
        functools.partial(_hgrn_sample_kernel, layer, t_len),
        grid=(nseq // bb,),
        in_specs=[
            pl.BlockSpec((bb * t_len, 4 * MIX_H), lambda i: (i, 0)),
            pl.BlockSpec((bb, H_HEADS, H_EXPAND, H_EXPAND), lambda i: (i, 0, 0, 0)),
            pl.BlockSpec(lb_logits.shape, const),
            pl.BlockSpec((1, MIX_H), const),
        ],
        out_specs=[
            pl.BlockSpec((bb * t_len, MIX_H), lambda i: (i, 0)),
            pl.BlockSpec((bb, H_HEADS, H_EXPAND, H_EXPAND), lambda i: (i, 0, 0, 0)),
        ],
        out_shape=[
            jax.ShapeDtypeStruct((n, MIX_H), BF16),
            jax.ShapeDtypeStruct(s0.shape, F32),
        ],
        compiler_params=pltpu.CompilerParams(
            dimension_semantics=("arbitrary",), vmem_limit_bytes=VMEM_LIMIT),
        name="hgrn_sample",
    )(zh, s0, lb_logits, norm_w)


def _attn_prompt_kernel(lam_init, tq, tk, q_ref, kt_ref, v_ref, lq1, lk1, lq2, lk2, sw_ref,
                        o_ref, m_ref, acc_ref):
    qi = pl.program_id(2)
    q = q_ref[...]
    lane = lax.broadcasted_iota(jnp.int32, (1, LANES), 1)
    qc = [jnp.where((lane // A_HEAD_DIM) == c, q, jnp.zeros_like(q)) for c in range(2)]
    m_ref[...] = jnp.full_like(m_ref, -jnp.inf)
    acc_ref[...] = jnp.zeros_like(acc_ref)

    def step(ki, diag):
        r0 = pl.multiple_of(ki * tk, tk)
        kt = kt_ref[:, pl.ds(r0, tk)]
        v = v_ref[pl.ds(r0, tk), :]
        rows = slice(0, tq) if diag is None else slice(diag * tk, tq)
        nr = rows.stop - rows.start
        for c in range(2):
            s = _dot(qc[c][rows], kt)
            if diag is not None:
                rr = lax.broadcasted_iota(jnp.int32, (nr, tk), 0)
                cc = lax.broadcasted_iota(jnp.int32, (nr, tk), 1)
                s = jnp.where(cc <= rr, s, -jnp.inf)
            m_prev = m_ref[c, rows]
            m_next = jnp.maximum(m_prev, jnp.max(s, axis=1, keepdims=True))
            alpha = jnp.exp2(m_prev - m_next)
            p = jnp.exp2(s - pltpu.repeat(m_next, tk // LANES, axis=1))
            a = acc_ref[c, rows]
            pv = _dot(p.astype(BF16), v)
            l = a[:, LANES:] * alpha + jnp.sum(p, axis=1, keepdims=True)
            acc_ref[c, rows] = jnp.concatenate([a[:, :LANES] * alpha + pv, l], axis=1)
            m_ref[c, rows] = m_next

    def body(ki, carry):
        step(ki, None)
        return carry

    band = tq // tk
    lax.fori_loop(0, qi * band, body, 0)
    for d in range(band):
        step(qi * band + d, d)

    lam = _lambda(lq1, lk1, lq2, lk2, lam_init)
    a0, a1 = acc_ref[0], acc_ref[1]
    o = a0[:, :LANES] / a0[:, LANES:] - lam * (a1[:, :LANES] / a1[:, LANES:])
    o_ref[...] = (_rms(o, sw_ref[...], SUBLN_EPS) * (1.0 - lam_init)).astype(BF16)


def _attn_prompt(q, kbt, vb, lams, subln_w, lam_init, batch, seq, tq, tk):
    n = q.shape[0]
    nq = seq // tq
    vb3 = vb.reshape(batch, seq, MIX_A)
    const = lambda b, h, i: (0, 0)
    return pl.pallas_call(
        functools.partial(_attn_prompt_kernel, lam_init, tq, tk),
        grid=(batch, A_HEADS, nq),
        in_specs=[
            pl.BlockSpec((tq, LANES), lambda b, h, i: (b * nq + i, h)),
            pl.BlockSpec((None, LANES, seq), lambda b, h, i: (b, h, 0)),
            pl.BlockSpec((None, seq, LANES), lambda b, h, i: (b, 0, h)),
        ] + [pl.BlockSpec((1, A_HEAD_DIM), const)] * 4 + [
            pl.BlockSpec((1, 2 * A_HEAD_DIM), const),
        ],
        out_specs=pl.BlockSpec((tq, LANES), lambda b, h, i: (b * nq + i, h)),
        out_shape=jax.ShapeDtypeStruct((n, MIX_A), BF16),
        scratch_shapes=[
            pltpu.VMEM((2, tq, LANES), F32),
            pltpu.VMEM((2, tq, 2 * LANES), F32),
        ],
        compiler_params=pltpu.CompilerParams(
            dimension_semantics=("arbitrary", "arbitrary", "arbitrary"),
            vmem_limit_bytes=VMEM_LIMIT),
        name="attn_prompt",
    )(q, kbt, vb3, *lams, subln_w)


def _attn_sample_kernel(lam_init, n_pages, t_len, pt_ref, q_ref, kn_ref, vn_ref, *rest):
    kp_refs = rest[:n_pages]
    vp_refs = rest[n_pages:2 * n_pages]
    lq1, lk1, lq2, lk2, sw_ref, o_ref = rest[2 * n_pages:]
    nrow = 2 * A_HEADS * t_len
    r = lax.broadcasted_iota(jnp.int32, (nrow, 1), 0)
    r_tok = r % t_len
    r_grp = (r // t_len) % A_HEADS * 2 + r // (A_HEADS * t_len)
    lane = lax.broadcasted_iota(jnp.int32, (1, MIX_A), 1)
    q4 = q_ref[...].astype(F32)
    qt = jnp.zeros((nrow, MIX_A), F32)
    for t in range(t_len):
        qt = jnp.where(r_tok == t, q4[t:t + 1, :], qt)
    qt = jnp.where(lane // A_HEAD_DIM == r_grp, qt, 0.0)
    qtb = qt.astype(BF16)

    s_past = [_dot(qtb, kp[...].astype(BF16)) for kp in kp_refs]
    kn = kn_ref[...]
    s_new = []
    for t in range(t_len):
        sn = jnp.sum(qt * kn[t:t + 1, :], axis=-1, keepdims=True)
        s_new.append(jnp.where(r_tok >= t, sn, -jnp.inf))
    m = s_new[0]
    for sp in s_past:
        m = jnp.maximum(m, jnp.max(sp, axis=-1, keepdims=True))
    for sn in s_new[1:]:
        m = jnp.maximum(m, sn)
    p_past = [jnp.exp2(sp - m) for sp in s_past]
    p_new = [jnp.exp2(sn - m) for sn in s_new]
    l = p_new[0]
    for pn in p_new[1:]:
        l = l + pn
    for pp in p_past:
        l = l + jnp.sum(pp, axis=-1, keepdims=True)
    inv = 1.0 / l
    lam = _lambda(lq1, lk1, lq2, lk2, lam_init)
    half = nrow // 2
    w = jnp.where(r < half, inv, -lam * inv)

    def combine(p):
        pw = p * w
        return pw[:half] + pw[half:]

    pd_past = jnp.concatenate([combine(pp) for pp in p_past], axis=1).astype(BF16)
    pd_new = [combine(pn) for pn in p_new]
    vn = vn_ref[...]
    page = vp_refs[0].shape[0] // A_HEADS
    outs = []
    for hh in range(A_HEADS):
        vh = jnp.concatenate(
            [vp[pl.ds(hh, page, stride=A_HEADS), :].astype(BF16) for vp in vp_refs], axis=0)
        hs = slice(hh * LANES, (hh + 1) * LANES)
        res = _dot(pd_past, vh)
        for t in range(t_len):
            res = res + pd_new[t] * vn[t:t + 1, hs]
        o = res[hh * t_len:(hh + 1) * t_len]
        outs.append(_rms(o, sw_ref[...], SUBLN_EPS) * (1.0 - lam_init))
    o_ref[...] = jnp.concatenate(outs, axis=1).astype(BF16)


def _attn_sample(q, k_new, v_new, cache_k, cache_v, page_table, lams, subln_w, lam_init, t_len):
    nseq, n_pages = page_table.shape
    n_pool, page = cache_k.shape[0], cache_k.shape[1]
    ck = jnp.transpose(cache_k, (0, 2, 3, 4, 1)).reshape(n_pool, MIX_A, page)
    cv = cache_v.reshape(n_pool, page * A_HEADS, LANES)
    q3 = q.reshape(nseq, t_len, MIX_A)
    k3 = k_new.reshape(nseq, t_len, MIX_A)
    v3 = v_new.reshape(nseq, t_len, MIX_A)
    tok = pl.BlockSpec((None, t_len, MIX_A), lambda b, pt: (b, 0, 0))
    const = lambda b, pt: (0, 0)
    kpage = lambda j: pl.BlockSpec((None, MIX_A, page), lambda b, pt: (pt[b, j], 0, 0))
    vpage = lambda j: pl.BlockSpec((None, page * A_HEADS, LANES), lambda b, pt: (pt[b, j], 0, 0))
    grid_spec = pltpu.PrefetchScalarGridSpec(
        num_scalar_prefetch=1,
        grid=(nseq,),
        in_specs=[tok, tok, tok]
        + [kpage(j) for j in range(n_pages)]
        + [vpage(j) for j in range(n_pages)]
        + [pl.BlockSpec((1, A_HEAD_DIM), const)] * 4
        + [pl.BlockSpec((1, 2 * A_HEAD_DIM), const)],
        out_specs=tok,
    )
    out = pl.pallas_call(
        functools.partial(_attn_sample_kernel, lam_init, n_pages, t_len),
        grid_spec=grid_spec,
        out_shape=jax.ShapeDtypeStruct((nseq, t_len, MIX_A), BF16),
        compiler_params=pltpu.CompilerParams(
            dimension_semantics=("arbitrary",), vmem_limit_bytes=VMEM_LIMIT),
        name="attn_sample",
    )(page_table, q3, k3, v3, *([ck] * n_pages), *([cv] * n_pages), *lams, subln_w)
    return out.reshape(nseq * t_len, MIX_A)


def _out_ffn_kernel(x_ref, ma_ref, mh_ref, wo_ref, fnw_ref, wg_ref, wu_ref, wd_ref, finw_ref, y_ref):
    x2 = x_ref[...] + _dot(ma_ref[...], wo_ref[0:MIX_A, :]) + _dot(mh_ref[...], wo_ref[MIX_A:, :])
    h2 = _rms(x2, fnw_ref[...], EPS).astype(BF16)
    acc = jnp.zeros_like(x2)
    for c in range(D_FF // FF_CHUNK):
        cs = slice(c * FF_CHUNK, (c + 1) * FF_CHUNK)
        g = _dot(h2, wg_ref[:, cs])
        u = _dot(h2, wu_ref[:, cs])
        acc = acc + _dot((g * _sigmoid(g) * u).astype(BF16), wd_ref[cs, :])
    y_ref[...] = _rms(x2 + acc, finw_ref[...], EPS)


def _out_ffn(x, mix_a, mix_h, wo, fnw, wg, wu, wd, finw, tm):
    n = x.shape[0]
    row = lambda i: (i, 0)
    const = lambda i: (0, 0)
    wspec = lambda shape: pl.BlockSpec(shape, const, pipeline_mode=pl.Buffered(1))
    return pl.pallas_call(
        _out_ffn_kernel,
        grid=(n // tm,),
        in_specs=[
            pl.BlockSpec((tm, D_MODEL), row),
            pl.BlockSpec((tm, MIX_A), row),
            pl.BlockSpec((tm, MIX_H), row),
            wspec((MIX_A + MIX_H, D_MODEL)),
            pl.BlockSpec((1, D_MODEL), const),
            wspec((D_MODEL, D_FF)),
            wspec((D_MODEL, D_FF)),
            wspec((D_FF, D_MODEL)),
            pl.BlockSpec((1, D_MODEL), const),
        ],
        out_specs=pl.BlockSpec((tm, D_MODEL), row),
        out_shape=jax.ShapeDtypeStruct((n, D_MODEL), F32),
        compiler_params=pltpu.CompilerParams(
            dimension_semantics=("arbitrary",), vmem_limit_bytes=VMEM_LIMIT),
        name="out_ffn",
    )(x, mix_a, mix_h, wo, fnw, wg, wu, wd, finw)


def _pick(n, prefs):
    for p in prefs:
        if n % p == 0:
            return p
    return n


def kernel(x_prompt, x_sample, cache_k, cache_v, state_hgrn, page_table, attn_norm, w_in, w_out,
           lambda_q1, lambda_k1, lambda_q2, lambda_k2, subln_w, hgrn_lb_logits, hgrn_norm,
           ffn_norm, w_gate, w_up, w_down, final_norm):
    batch, seq, _ = x_prompt.shape
    nseq, t_len, _ = x_sample.shape
    n_pages = page_table.shape[1]
    page = cache_k.shape[2]
    depth = w_in.shape[0]
    assert depth == 1, "single-layer step"
    past_len = n_pages * page
    layer = 0
    lam_init = 0.8 - 0.6 * math.exp(-0.3 * layer)

    row2 = lambda a: a.reshape(1, -1)
    w_in_bf = w_in[layer].astype(BF16)
    wo = w_out[layer].astype(BF16)
    wg = w_gate[layer].astype(BF16)
    wu = w_up[layer].astype(BF16)
    wd = w_down[layer].astype(BF16)
    lams = [row2(a[layer]) for a in (lambda_q1, lambda_k1, lambda_q2, lambda_k2)]
    sw = row2(subln_w[layer])
    an, hn, fn, finw = row2(attn_norm[layer]), row2(hgrn_norm[layer]), row2(ffn_norm[layer]), row2(final_norm)

    n_p = batch * seq
    xp = x_prompt.reshape(n_p, D_MODEL)
    tm_p = _pick(seq, (512, 256, 128, 64))
    wkt_bf = w_in[layer][:, MIX_A:2 * MIX_A].T.astype(BF16)
    q, kt_new, kbt, v_new, vb, zh = _inproj_prompt(
        xp, an, w_in_bf, wkt_bf, _rope_tables(jnp.arange(seq)), batch, seq, tm_p)
    mix_h, s_prompt = _hgrn_prompt(zh, hgrn_lb_logits, hn, layer, batch, seq, _pick(seq, (512, 256, 128, 64)))
    tk_a = _pick(seq, (512, 256, 128))
    tq_a = _pick(seq, (2 * tk_a, tk_a))
    mix_a = _attn_prompt(q, kbt, vb, lams, sw, lam_init, batch, seq, tq_a, tk_a)
    y_prompt = _out_ffn(xp, mix_a, mix_h, wo, fn, wg, wu, wd, finw,
                        _pick(seq, (512, 256, 128, 64))).reshape(batch, seq, D_MODEL)
    k_new = jnp.transpose(kt_new.reshape(batch, A_HEADS, 2, A_HEAD_DIM, seq), (0, 4, 1, 2, 3))

    n_s = nseq * t_len
    xs = x_sample.reshape(n_s, D_MODEL)
    tm_s = _pick(n_s, (256, 128, 64, 32, 16, 8))
    qs, ks_new, vs_new, zhs = _inproj_sample(
        xs, an, w_in_bf, _rope_tables(past_len + jnp.arange(n_s) % t_len)[:3], tm_s)
    bb = _pick(nseq, (8, 4, 2, 1))
    mix_h_s, s_sample = _hgrn_sample(zhs, state_hgrn[layer].astype(F32), hgrn_lb_logits, hn, layer, t_len, bb)
    mix_a_s = _attn_sample(qs, ks_new, vs_new, cache_k[layer], cache_v[layer], page_table, lams, sw,
                           lam_init, t_len)
    y_sample = _out_ffn(xs, mix_a_s, mix_h_s, wo, fn, wg, wu, wd, finw, tm_s).reshape(nseq, t_len, D_MODEL)

    return (y_prompt, y_sample,
            k_new[None],
            v_new.reshape(1, batch, seq, A_HEADS, 2 * A_HEAD_DIM),
            s_prompt[None],
            ks_new.reshape(1, nseq, t_len, A_HEADS, 2, A_HEAD_DIM),
            vs_new.reshape(1, nseq, t_len, A_HEADS, 2 * A_HEAD_DIM),
            s_sample[None])
```
